```python
import math
import jax, jax.numpy as jnp
from jax import lax
import numpy as np

D_MODEL = 1024
BATCH = 4
SEQ = 4096
DEPTH = 2
DEC_BATCH = 32
DEC_SEQ = 64
PAST_LEN = 2048

CHUNK = 64
MIX_WIDTH = D_MODEL
C_CONV = MIX_WIDTH // 2
CONV_W = 31
GLA_HEADS = 4
GLA_DV = (MIX_WIDTH // 2) // GLA_HEADS
GLA_DK = GLA_DV // 2
GATE_RANK = 16
GATE_TAU = 16.0
D_FF = 2816
EPS = 1e-6

QK_W = GLA_HEADS * GLA_DK
V_W = GLA_HEADS * GLA_DV
IN_W = 2 * C_CONV + 2 * QK_W + 2 * V_W + GATE_RANK
SPLITS = (C_CONV, 2 * C_CONV, 2 * C_CONV + QK_W, 2 * C_CONV + 2 * QK_W,
          2 * C_CONV + 2 * QK_W + V_W, 2 * C_CONV + 2 * QK_W + 2 * V_W)

kernel_name = "hymba_conformer_gla_stream_step"


def rmsnorm(x, g):
    xf = x.astype(jnp.float32)
    y = xf * lax.rsqrt(jnp.mean(xf * xf, axis=-1, keepdims=True) + EPS)
    return (y * g.astype(jnp.float32)).astype(x.dtype)


def layernorm(x, g, b):
    xf = x.astype(jnp.float32)
    mu = jnp.mean(xf, axis=-1, keepdims=True)
    xc = xf - mu
    y = xc * lax.rsqrt(jnp.mean(xc * xc, axis=-1, keepdims=True) + EPS)
    return (y * g.astype(jnp.float32) + b.astype(jnp.float32)).astype(x.dtype)


def swiglu(x, wg, wu, wd):
    return (jax.nn.silu(x @ wg) * (x @ wu)) @ wd


def conv_module(u_a, u_b, conv_cache, dw_w, dw_b, ln_g, ln_b):
    z = u_a * jax.nn.sigmoid(u_b)
    zp = jnp.concatenate([conv_cache.astype(z.dtype), z], axis=1)
    y = lax.conv_general_dilated(zp, dw_w[:, None, :].astype(z.dtype), window_strides=(1,),
                                 padding='VALID', dimension_numbers=('NWC', 'WIO', 'NWC'),
                                 feature_group_count=C_CONV) + dw_b
    y = jax.nn.silu(layernorm(y, ln_g, ln_b))
    return y, zp[:, -(CONV_W - 1):]


def gla_chunk_step(S, inp):
    q, k, v, la = inp
    b = jnp.cumsum(la, axis=1)
    o_inter = jnp.einsum('bchk,bhkv->bchv', q * jnp.exp(b), S)
    c = q.shape[1]
    causal = (jnp.arange(c)[:, None] >= jnp.arange(c)[None, :])[None, :, :, None, None]
    diff = b[:, :, None] - b[:, None, :]
    decay = jnp.exp(jnp.where(causal, diff, -jnp.inf))
    att = jnp.einsum('bihk,bjhk,bijhk->bhij', q, k, decay)
    o_intra = jnp.einsum('bhij,bjhv->bihv', att, v)
    b_last = b[:, -1]
    k_dec = k * jnp.exp(b_last[:, None] - b)
    S_new = jnp.exp(b_last)[..., None] * S + jnp.einsum('bchk,bchv->bhkv', k_dec, v)
    return S_new, o_inter + o_intra


def gla_scan(q, k, v, la, S0):
    B, L = q.shape[0], q.shape[1]
    n = -(-L // CHUNK)
    pad = n * CHUNK - L
    padf = lambda a: jnp.pad(a.astype(jnp.float32), ((0, 0), (0, pad), (0, 0), (0, 0)))
    to_chunks = lambda a: jnp.moveaxis(a.reshape(B, n, CHUNK, a.shape[2], a.shape[3]), 1, 0)
    xs = tuple(to_chunks(padf(a)) for a in (q, k, v, la))
    S_fin, o = lax.scan(gla_chunk_step, S0.astype(jnp.float32), xs)
    o = jnp.moveaxis(o, 0, 1).reshape(B, n * CHUNK, GLA_HEADS, GLA_DV)[:, :L]
    return o, S_fin


def mixer(h, conv_cache, gla_state, w_in, conv_dw_w, conv_dw_b, conv_ln_g, conv_ln_b,
          gla_gate_w2, gla_gate_b, gla_norm_g, w_out):
    B, L, _ = h.shape
    u = h @ w_in
    u_a, u_b, q, k, v, g, gz = jnp.split(u, SPLITS, axis=-1)
    y_conv, new_cc = conv_module(u_a, u_b, conv_cache, conv_dw_w, conv_dw_b, conv_ln_g, conv_ln_b)
    q = q.reshape(B, L, GLA_HEADS, GLA_DK) * (GLA_DK ** -0.5)
    k = k.reshape(B, L, GLA_HEADS, GLA_DK)
    v = v.reshape(B, L, GLA_HEADS, GLA_DV)
    la = jax.nn.log_sigmoid((gz @ gla_gate_w2 + gla_gate_b).astype(jnp.float32)) / GATE_TAU
    la = la.reshape(B, L, GLA_HEADS, GLA_DK)
    o, new_S = gla_scan(q, k, v, la, gla_state)
    o = rmsnorm(o, gla_norm_g.reshape(GLA_HEADS, GLA_DV)).astype(h.dtype)
    o = o.reshape(B, L, V_W) * jax.nn.silu(g)
    out = jnp.concatenate([y_conv, o], axis=-1) @ w_out
    return out, new_cc, new_S


def trunk(x, conv_caches, gla_states, params):
    (ffn1_norm, ffn1_w_gate, ffn1_w_up, ffn1_w_down, mix_norm, w_in, conv_dw_w, conv_dw_b,
     conv_ln_g, conv_ln_b, gla_gate_w2, gla_gate_b, gla_norm_g, w_out, ffn2_norm,
     ffn2_w_gate, ffn2_w_up, ffn2_w_down, final_norm) = params
    new_cc, new_S = [], []
    for l in range(DEPTH):
        x = x + 0.5 * swiglu(rmsnorm(x, ffn1_norm[l]), ffn1_w_gate[l], ffn1_w_up[l], ffn1_w_down[l])
        m, cc, S = mixer(rmsnorm(x, mix_norm[l]), conv_caches[l], gla_states[l], w_in[l],
                         conv_dw_w[l], conv_dw_b[l], conv_ln_g[l], conv_ln_b[l],
                         gla_gate_w2[l], gla_gate_b[l], gla_norm_g[l], w_out[l])
        x = x + m
        x = x + 0.5 * swiglu(rmsnorm(x, ffn2_norm[l]), ffn2_w_gate[l], ffn2_w_up[l], ffn2_w_down[l])
        new_cc.append(cc)
        new_S.append(S)
    return rmsnorm(x, final_norm), jnp.stack(new_cc), jnp.stack(new_S)


def setup_inputs(seed: int = 0) -> dict:
    key = jax.random.key(seed)
    ks = iter(jax.random.split(key, 32))
    nrm = lambda shape, s: jax.random.normal(next(ks), shape, jnp.float32) * s
    gain = lambda shape: 1.0 + nrm(shape, 0.02)
    return {
        "x_prompt": nrm((BATCH, SEQ, D_MODEL), 1.0),
        "x_sample": nrm((DEC_BATCH, DEC_SEQ, D_MODEL), 1.0),
        "cache_conv": nrm((DEPTH, DEC_BATCH, CONV_W - 1, C_CONV), 0.5),
        "state_gla": nrm((DEPTH, DEC_BATCH, GLA_HEADS, GLA_DK, GLA_DV), 2.0),
        "ffn1_norm": gain((DEPTH, D_MODEL)),
        "ffn1_w_gate": nrm((DEPTH, D_MODEL, D_FF), D_MODEL ** -0.5),
        "ffn1_w_up": nrm((DEPTH, D_MODEL, D_FF), D_MODEL ** -0.5),
        "ffn1_w_down": nrm((DEPTH, D_FF, D_MODEL), D_FF ** -0.5),
        "mix_norm": gain((DEPTH, D_MODEL)),
        "w_in": nrm((DEPTH, D_MODEL, IN_W), D_MODEL ** -0.5),
        "conv_dw_w": nrm((DEPTH, CONV_W, C_CONV), CONV_W ** -0.5),
        "conv_dw_b": nrm((DEPTH, C_CONV), 0.02),
        "conv_ln_g": gain((DEPTH, C_CONV)),
        "conv_ln_b": nrm((DEPTH, C_CONV), 0.02),
        "gla_gate_w2": nrm((DEPTH, GATE_RANK, QK_W), GATE_RANK ** -0.5),
        "gla_gate_b": nrm((DEPTH, QK_W), 0.1),
        "gla_norm_g": gain((DEPTH, V_W)),
        "w_out": nrm((DEPTH, MIX_WIDTH, D_MODEL), MIX_WIDTH ** -0.5),
        "ffn2_norm": gain((DEPTH, D_MODEL)),
        "ffn2_w_gate": nrm((DEPTH, D_MODEL, D_FF), D_MODEL ** -0.5),
        "ffn2_w_up": nrm((DEPTH, D_MODEL, D_FF), D_MODEL ** -0.5),
        "ffn2_w_down": nrm((DEPTH, D_FF, D_MODEL), D_FF ** -0.5),
        "final_norm": gain((D_MODEL,)),
    }


def reference(x_prompt, x_sample, cache_conv, state_gla, ffn1_norm, ffn1_w_gate, ffn1_w_up,
              ffn1_w_down, mix_norm, w_in, conv_dw_w, conv_dw_b, conv_ln_g, conv_ln_b,
              gla_gate_w2, gla_gate_b, gla_norm_g, w_out, ffn2_norm, ffn2_w_gate, ffn2_w_up,
              ffn2_w_down, final_norm):
    params = (ffn1_norm, ffn1_w_gate, ffn1_w_up, ffn1_w_down, mix_norm, w_in, conv_dw_w,
              conv_dw_b, conv_ln_g, conv_ln_b, gla_gate_w2, gla_gate_b, gla_norm_g, w_out,
              ffn2_norm, ffn2_w_gate, ffn2_w_up, ffn2_w_down, final_norm)
    B = x_prompt.shape[0]
    zero_cc = jnp.zeros((DEPTH, B, CONV_W - 1, C_CONV), x_prompt.dtype)
    zero_S = jnp.zeros((DEPTH, B, GLA_HEADS, GLA_DK, GLA_DV), jnp.float32)
    y_prompt, conv_prompt, gla_prompt = trunk(x_prompt, zero_cc, zero_S, params)
    y_sample, conv_sample, gla_sample = trunk(x_sample, cache_conv, state_gla, params)
    return (y_prompt, y_sample, conv_prompt, gla_prompt, conv_sample, gla_sample)
```

```python
import functools

import jax
import jax.numpy as jnp
from jax import lax
from jax.experimental import pallas as pl
from jax.experimental.pallas import tpu as pltpu

D_MODEL = 1024
DEPTH = 2
C_CONV = 512
CONV_W = 31
HEADS = 4
DK = 64
DV = 128
QK_W = HEADS * DK
V_W = HEADS * DV
GATE_RANK = 16
GATE_TAU = 16.0
D_FF = 2816
EPS = 1e-6
CHUNK = 64
SUB = 16
NSUB = CHUNK // SUB

LANES = 128
HIST = 32
FF_CHUNK = 256
VMEM_LIMIT = 58 * 1024 * 1024

F32 = jnp.float32
BF16 = jnp.bfloat16


def _sigmoid(x):
    return 1.0 / (1.0 + jnp.exp(-x))


def _rms(x, g):
    return x * lax.rsqrt(jnp.mean(x * x, axis=-1, keepdims=True) + EPS) * g


def _dot(a, b):
    return jnp.dot(a, b, preferred_element_type=F32)


def _swiglu(h, wg_ref, wu_ref, wd_ref, a_ref):
    for c in range(D_FF // FF_CHUNK):
        sl = slice(c * FF_CHUNK, (c + 1) * FF_CHUNK)
        g = _dot(h, wg_ref[:, sl])
        u = _dot(h, wu_ref[:, sl])
        a_ref[:, sl] = (g * _sigmoid(g) * u).astype(BF16)
    return _dot(a_ref[...], wd_ref[...])


def _ffn_inproj_kernel(x_ref, n1_ref, wg_ref, wu_ref, wd_ref, n2_ref, win_ref, wgz_ref, w2_ref, gb_ref,
                       x1_ref, z_ref, qk_ref, v_ref, sg_ref, la_ref, a_ref):
    x = x_ref[...]
    h = _rms(x, n1_ref[...]).astype(BF16)
    x1 = x + 0.5 * _swiglu(h, wg_ref, wu_ref, wd_ref, a_ref)
    x1_ref[...] = x1
    h2 = _rms(x1, n2_ref[...]).astype(BF16)
    ua = _dot(h2, win_ref[:, 0:C_CONV])
    ub = _dot(h2, win_ref[:, C_CONV:2 * C_CONV])
    z_ref[...] = ua * _sigmoid(ub)
    o = 2 * C_CONV
    qk_ref[:, 0:QK_W] = _dot(h2, win_ref[:, o:o + QK_W]) * (DK ** -0.5)
    qk_ref[:, QK_W:2 * QK_W] = _dot(h2, win_ref[:, o + QK_W:o + 2 * QK_W])
    o += 2 * QK_W
    v_ref[...] = _dot(h2, win_ref[:, o:o + V_W]).astype(BF16)
    g = _dot(h2, win_ref[:, o + V_W:o + 2 * V_W])
    sg_ref[...] = g * _sigmoid(g)
    gz = _dot(h2, wgz_ref[...])
    pre = jnp.dot(gz, w2_ref[...], preferred_element_type=F32, precision=lax.Precision.HIGHEST) + gb_ref[...]
    la_ref[...] = (jnp.minimum(pre, 0.0) - jnp.log(1.0 + jnp.exp(-jnp.abs(pre)))) * (1.0 / GATE_TAU)


def _full(shape):
    return pl.BlockSpec(memory_space=pltpu.VMEM)


def _ffn_inproj(x, n1, wg, wu, wd, n2, win, wgz, w2, gb, tm):
    m = x.shape[0]
    row = lambda c: pl.BlockSpec((tm, c), lambda i: (i, 0))
    outs = [(D_MODEL, F32), (C_CONV, F32), (2 * QK_W, F32), (V_W, BF16), (V_W, F32), (QK_W, F32)]
    return pl.pallas_call(
        _ffn_inproj_kernel,
        grid=(m // tm,),
        in_specs=[row(D_MODEL)] + [_full(None)] * 9,
        out_specs=[row(c) for c, _ in outs],
        out_shape=[jax.ShapeDtypeStruct((m, c), dt) for c, dt in outs],
        scratch_shapes=[pltpu.VMEM((tm, D_FF), BF16)],
        compiler_params=pltpu.CompilerParams(dimension_semantics=("arbitrary",), vmem_limit_bytes=VMEM_LIMIT),
        name="ffn_inproj",
    )(x, n1, wg, wu, wd, n2, win, wgz, w2, gb)


def _outproj_ffn_kernel(x1_ref, m_ref, wo_ref, n3_ref, wg_ref, wu_ref, wd_ref, nf_ref, o_ref, a_ref, *, final):
    x2 = x1_ref[...] + _dot(m_ref[...], wo_ref[...])
    h = _rms(x2, n3_ref[...]).astype(BF16)
    x3 = x2 + 0.5 * _swiglu(h, wg_ref, wu_ref, wd_ref, a_ref)
    o_ref[...] = _rms(x3, nf_ref[...]) if final else x3


def _outproj_ffn(x1, mix, wo, n3, wg, wu, wd, nf, tm, final):
    m = x1.shape[0]
    row = lambda c: pl.BlockSpec((tm, c), lambda i: (i, 0))
    return pl.pallas_call(
        functools.partial(_outproj_ffn_kernel, final=final),
        grid=(m // tm,),
        in_specs=[row(D_MODEL), row(D_MODEL)] + [_full(None)] * 6,
        out_specs=row(D_MODEL),
        out_shape=jax.ShapeDtypeStruct((m, D_MODEL), F32),
        scratch_shapes=[pltpu.VMEM((tm, D_FF), BF16)],
        compiler_params=pltpu.CompilerParams(dimension_semantics=("arbitrary",), vmem_limit_bytes=VMEM_LIMIT),
        name="outproj_ffn",
    )(x1, mix, wo, n3, wg, wu, wd, nf)


def _head_of_row(n_rows, rows_per_head):
    return lax.broadcasted_iota(jnp.int32, (n_rows, QK_W), 0) // rows_per_head % HEADS


def _gla_chunk(q, k, v, la, s):
    r = lax.broadcasted_iota(jnp.int32, (CHUNK, CHUNK), 0)
    c = lax.broadcasted_iota(jnp.int32, (CHUNK, CHUNK), 1)
    tril = (r >= c).astype(BF16)
    la_hi = la.astype(BF16)
    r1 = la - la_hi.astype(F32)
    la_mid = r1.astype(BF16)
    la_lo = (r1 - la_mid.astype(F32)).astype(BF16)
    b = _dot(tril, la_hi) + _dot(tril, la_mid) + _dot(tril, la_lo)
    b_last = b[CHUNK - 1:CHUNK, :]

    lane_head = lax.broadcasted_iota(jnp.int32, (CHUNK, QK_W), 1) // DK
    head_mask = lane_head == _head_of_row(CHUNK, SUB)
    j_idx = lax.broadcasted_iota(jnp.int32, (CHUNK, QK_W), 0)
    row_local = lax.broadcasted_iota(jnp.int32, (CHUNK, CHUNK), 0) % SUB
    col_j = lax.broadcasted_iota(jnp.int32, (CHUNK, CHUNK), 1)

    qe = q * jnp.exp(b)
    p_blocks, qe_blocks = [], []
    for i in range(NSUB):
        rows = slice(i * SUB, (i + 1) * SUB)
        b_ref = b[i * SUB - 1:i * SUB, :] if i > 0 else jnp.zeros((1, QK_W), F32)
        qt = q[rows] * jnp.exp(b[rows] - b_ref)
        qx = jnp.where(head_mask, jnp.concatenate([qt] * HEADS, axis=0), 0.0).astype(BF16)
        kx = jnp.where(j_idx < (i + 1) * SUB, k * jnp.exp(b_ref - b), 0.0).astype(BF16)
        a = lax.dot_general(qx, kx, (((1,), (1,)), ((), ())), preferred_element_type=F32)
        p_blocks.append(jnp.where(col_j <= i * SUB + row_local, a, 0.0).astype(BF16))
        qe_blocks.append(jnp.where(head_mask, jnp.concatenate([qe[rows]] * HEADS, axis=0), 0.0).astype(BF16))
    p_all = jnp.concatenate(p_blocks, axis=0)
    qe_all = jnp.concatenate(qe_blocks, axis=0)
    intra = _dot(p_all, v)
    inter = _dot(qe_all, s.astype(BF16))
    o_rows = []
    for i in range(NSUB):
        blocks = []
        for h in range(HEADS):
            r0 = i * CHUNK + h * SUB
            blocks.append(intra[r0:r0 + SUB, h * DV:(h + 1) * DV] + inter[r0:r0 + SUB, :])
        o_rows.append(jnp.concatenate(blocks, axis=1))
    o = jnp.concatenate(o_rows, axis=0)

    kd = (k * jnp.exp(b_last - b)).astype(BF16)
    upd = lax.dot_general(kd, v, (((0,), (0,)), ((), ())), preferred_element_type=F32)
    upd = jnp.concatenate([upd[h * DK:(h + 1) * DK, h * DV:(h + 1) * DV] for h in range(HEADS)], axis=0)
    decay = jnp.exp(jnp.broadcast_to(b_last, (DV, QK_W)).T)
    return o, decay * s + upd


def _mixer_kernel(z_ref, qk_ref, v_ref, sg_ref, la_ref, cc_ref, s0_ref, dw_ref, dwb_ref, lng_ref, lnb_ref, gng_ref,
                  m_ref, s_ref, zbuf, *, nb, t):
    @pl.when(pl.program_id(1) == 0)
    def _():
        zbuf[:, 0:HIST, :] = cc_ref[...]
        s_ref[...] = s0_ref[...]

    gng = gng_ref[...]
    for bi in range(nb):
        zbuf[bi, HIST:HIST + t, :] = z_ref[bi]
        for c in range(t // CHUNK):
            rows = slice(c * CHUNK, (c + 1) * CHUNK)
            acc = jnp.broadcast_to(dwb_ref[...], (CHUNK, C_CONV))
            for w in range(CONV_W):
                start = c * CHUNK + HIST - (CONV_W - 1) + w
                acc = acc + zbuf[bi, start:start + CHUNK, :] * dw_ref[w:w + 1, :]
            mu = jnp.mean(acc, axis=-1, keepdims=True)
            xc = acc - mu
            yn = xc * lax.rsqrt(jnp.mean(xc * xc, axis=-1, keepdims=True) + EPS) * lng_ref[...] + lnb_ref[...]
            m_ref[bi, rows, 0:C_CONV] = (yn * _sigmoid(yn)).astype(BF16)

            o, s_new = _gla_chunk(qk_ref[bi, rows, 0:QK_W], qk_ref[bi, rows, QK_W:2 * QK_W], v_ref[bi, rows, :],
                                  la_ref[bi, rows, :], s_ref[bi])
            s_ref[bi] = s_new
            for h in range(HEADS):
                hs = slice(h * DV, (h + 1) * DV)
                oh = o[:, hs]
                on = oh * lax.rsqrt(jnp.mean(oh * oh, axis=-1, keepdims=True) + EPS) * gng[:, hs]
                m_ref[bi, rows, C_CONV + h * DV:C_CONV + (h + 1) * DV] = (on * sg_ref[bi, rows, hs]).astype(BF16)
        zbuf[bi, 0:HIST, :] = zbuf[bi, t:t + HIST, :]


def _mixer(z, qk, v, sg, la, cc, s0, dw, dwb, lng, lnb, gng, nb, t):
    bsz, length, _ = z.shape
    seq = lambda c: pl.BlockSpec((nb, t, c), lambda b, l: (b, l, 0))
    per_seq = lambda r, c: pl.BlockSpec((nb, r, c), lambda b, l: (b, 0, 0))
    return pl.pallas_call(
        functools.partial(_mixer_kernel, nb=nb, t=t),
        grid=(bsz // nb, length // t),
        in_specs=[seq(C_CONV), seq(2 * QK_W), seq(V_W), seq(V_W), seq(QK_W),
                  per_seq(HIST, C_CONV), per_seq(QK_W, DV)] + [_full(None)] * 5,
        out_specs=[seq(D_MODEL), per_seq(QK_W, DV)],
        out_shape=[jax.ShapeDtypeStruct((bsz, length, D_MODEL), BF16),
                   jax.ShapeDtypeStruct((bsz, QK_W, DV), F32)],
        scratch_shapes=[pltpu.VMEM((nb, HIST + t, C_CONV), F32)],
        compiler_params=pltpu.CompilerParams(dimension_semantics=("arbitrary", "arbitrary"),
                                             vmem_limit_bytes=VMEM_LIMIT),
        name="mixer",
    )(z, qk, v, sg, la, cc, s0, dw, dwb, lng, lnb, gng)


def _trunk(x, cc, s, w, tm, nb, t):
    bsz, length, _ = x.shape
    xf = x.reshape(bsz * length, D_MODEL)
    new_cc, new_s = [], []
    for l in range(DEPTH):
        x1, z, qk, v, sg, la = _ffn_inproj(xf, w["n1"][l], w["wg1"][l], w["wu1"][l], w["wd1"][l], w["n2"][l],
                                           w["win"][l], w["wgz"][l], w["w2"][l], w["gb"][l], tm)
        seq = lambda a: a.reshape(bsz, length, a.shape[-1])
        cc_l = jnp.pad(cc[l], ((0, 0), (HIST - (CONV_W - 1), 0), (0, 0)))
        mix, s_l = _mixer(seq(z), seq(qk), seq(v), seq(sg), seq(la), cc_l, s[l].reshape(bsz, QK_W, DV),
                          w["dw"][l], w["dwb"][l], w["lng"][l], w["lnb"][l], w["gng"][l], nb, t)
        new_cc.append(seq(z)[:, length - (CONV_W - 1):, :])
        new_s.append(s_l.reshape(bsz, HEADS, DK, DV))
        xf = _outproj_ffn(x1, mix.reshape(bsz * length, D_MODEL), w["wo"][l], w["n3"][l], w["wg2"][l],
                          w["wu2"][l], w["wd2"][l], w["nf"], tm, final=(l == DEPTH - 1))
    return xf.reshape(bsz, length, D_MODEL), jnp.stack(new_cc), jnp.stack(new_s)


def kernel(x_prompt, x_sample, cache_conv, state_gla, ffn1_norm, ffn1_w_gate, ffn1_w_up, ffn1_w_down, mix_norm, w_in, conv_dw_w, conv_dw_b, conv_ln_g, conv_ln_b, gla_gate_w2, gla_gate_b, gla_norm_g, w_out, ffn2_norm, ffn2_w_gate, ffn2_w_up, ffn2_w_down, final_norm):
    row = lambda a: a.reshape(DEPTH, 1, a.shape[-1])
    n_main = 2 * C_CONV + 2 * QK_W + 2 * V_W
    w = dict(
        n1=row(ffn1_norm), wg1=ffn1_w_gate.astype(BF16), wu1=ffn1_w_up.astype(BF16), wd1=ffn1_w_down.astype(BF16),
        n2=row(mix_norm), win=w_in[:, :, :n_main].astype(BF16),
        wgz=jnp.pad(w_in[:, :, n_main:], ((0, 0), (0, 0), (0, LANES - GATE_RANK))).astype(BF16),
        w2=jnp.pad(gla_gate_w2, ((0, 0), (0, LANES - GATE_RANK), (0, 0))), gb=row(gla_gate_b),
        dw=jnp.pad(conv_dw_w, ((0, 0), (0, 1), (0, 0))), dwb=row(conv_dw_b), lng=row(conv_ln_g), lnb=row(conv_ln_b),
        gng=row(gla_norm_g), wo=w_out.astype(BF16), n3=row(ffn2_norm), wg2=ffn2_w_gate.astype(BF16),
        wu2=ffn2_w_up.astype(BF16), wd2=ffn2_w_down.astype(BF16), nf=final_norm.reshape(1, D_MODEL),
    )
    bp = x_prompt.shape[0]
    zero_cc = jnp.zeros((DEPTH, bp, CONV_W - 1, C_CONV), F32)
    zero_s = jnp.zeros((DEPTH, bp, HEADS, DK, DV), F32)
    y_p, cc_p, s_p = _trunk(x_prompt, zero_cc, zero_s, w, tm=512, nb=4, t=128)
    y_s, cc_s, s_s = _trunk(x_sample, cache_conv, state_gla, w, tm=512, nb=4, t=64)
    return (y_p, y_s, cc_p, s_p, cc_s, s_s)
```

```python
import functools

import jax
import jax.numpy as jnp
from jax import lax
from jax.experimental import pallas as pl
from jax.experimental.pallas import tpu as pltpu

D_MODEL = 1024
DEPTH = 2
C_CONV = 512
CONV_W = 31
HEADS = 4
DK = 64
DV = 128
QK_W = HEADS * DK
V_W = HEADS * DV
GATE_RANK = 16
GATE_TAU = 16.0
D_FF = 2816
EPS = 1e-6
CHUNK = 64
SUB = 16
NSUB = CHUNK // SUB

LANES = 128
N_SLAB = C_CONV // LANES
HIST = 32
FF_CHUNK = 256
TM = 512
VMEM_LIMIT = 58 * 1024 * 1024

F32 = jnp.float32
BF16 = jnp.bfloat16


def _sigmoid(x):
    return 1.0 / (1.0 + jnp.exp(-x))


def _rms(x, g):
    return x * lax.rsqrt(jnp.mean(x * x, axis=-1, keepdims=True) + EPS) * g


def _dot(a, b):
    return jnp.dot(a, b, preferred_element_type=F32)


def _swiglu(h, wg_ref, wu_ref, wd_ref, a_ref):
    for c in range(D_FF // FF_CHUNK):
        sl = slice(c * FF_CHUNK, (c + 1) * FF_CHUNK)
        g = _dot(h, wg_ref[:, sl])
        u = _dot(h, wu_ref[:, sl])
        a_ref[:, sl] = (g * _sigmoid(g) * u).astype(BF16)
    return _dot(a_ref[...], wd_ref[...])


_VMEM = pl.BlockSpec(memory_space=pltpu.VMEM)


def _gla_chunk(q, k, v, la, s):
    r = lax.broadcasted_iota(jnp.int32, (CHUNK, CHUNK), 0)
    c = lax.broadcasted_iota(jnp.int32, (CHUNK, CHUNK), 1)
    tril = (r >= c).astype(BF16)
    la_hi = la.astype(BF16)
    r1 = la - la_hi.astype(F32)
    la_mid = r1.astype(BF16)
    la_lo = (r1 - la_mid.astype(F32)).astype(BF16)
    b = _dot(tril, la_hi) + _dot(tril, la_mid) + _dot(tril, la_lo)
    b_last = b[CHUNK - 1:CHUNK, :]

    lane_head = lax.broadcasted_iota(jnp.int32, (CHUNK, QK_W), 1) // DK
    row_head = lax.broadcasted_iota(jnp.int32, (CHUNK, QK_W), 0) // SUB
    head_mask = lane_head == row_head
    j_idx = lax.broadcasted_iota(jnp.int32, (CHUNK, QK_W), 0)
    row_local = r % SUB

    qe = q * jnp.exp(b)
    p_blocks, qe_blocks = [], []
    for i in range(NSUB):
        rows = slice(i * SUB, (i + 1) * SUB)
        b_ref = b[i * SUB - 1:i * SUB, :] if i > 0 else jnp.zeros((1, QK_W), F32)
        qt = q[rows] * jnp.exp(b[rows] - b_ref)
        qx = jnp.where(head_mask, jnp.concatenate([qt] * HEADS, axis=0), 0.0).astype(BF16)
        kx = jnp.where(j_idx < (i + 1) * SUB, k * jnp.exp(b_ref - b), 0.0).astype(BF16)
        a = lax.dot_general(qx, kx, (((1,), (1,)), ((), ())), preferred_element_type=F32)
        p_blocks.append(jnp.where(c <= i * SUB + row_local, a, 0.0).astype(BF16))
        qe_blocks.append(jnp.where(head_mask, jnp.concatenate([qe[rows]] * HEADS, axis=0), 0.0).astype(BF16))
    p_all = jnp.concatenate(p_blocks, axis=0)
    qe_all = jnp.concatenate(qe_blocks, axis=0)
    intra = _dot(p_all, v)
    inter = _dot(qe_all, s.astype(BF16))
    o_rows = []
    for i in range(NSUB):
        blocks = []
        for h in range(HEADS):
            r0 = i * CHUNK + h * SUB
            blocks.append(intra[r0:r0 + SUB, h * DV:(h + 1) * DV] + inter[r0:r0 + SUB, :])
        o_rows.append(jnp.concatenate(blocks, axis=1))
    o = jnp.concatenate(o_rows, axis=0)

    kd = (k * jnp.exp(b_last - b)).astype(BF16)
    upd = lax.dot_general(kd, v, (((0,), (0,)), ((), ())), preferred_element_type=F32)
    upd = jnp.concatenate([upd[h * DK:(h + 1) * DK, h * DV:(h + 1) * DV] for h in range(HEADS)], axis=0)
    decay = jnp.exp(jnp.broadcast_to(b_last, (DV, QK_W)).T)
    return o, decay * s + upd


def _mix_chunk(zrow, rows, s, zbuf, qk_scr, v_scr, sg_scr, la_scr, dw_ref, dwb_ref, lng_ref, lnb_ref, gng_ref, m_ref):
    acc = []
    for j in range(N_SLAB):
        ls = slice(j * LANES, (j + 1) * LANES)
        a = jnp.broadcast_to(dwb_ref[:, ls], (CHUNK, LANES))
        for w in range(CONV_W):
            a = a + zbuf[j, pl.ds(zrow - (CONV_W - 1) + w, CHUNK, stride=1), :] * dw_ref[w:w + 1, ls]
        acc.append(a)
    mu = sum(jnp.sum(a, axis=-1, keepdims=True) for a in acc) * (1.0 / C_CONV)
    xc = [a - mu for a in acc]
    rstd = lax.rsqrt(sum(jnp.sum(x * x, axis=-1, keepdims=True) for x in xc) * (1.0 / C_CONV) + EPS)
    for j in range(N_SLAB):
        ls = slice(j * LANES, (j + 1) * LANES)
        yn = xc[j] * rstd * lng_ref[:, ls] + lnb_ref[:, ls]
        m_ref[rows, ls] = (yn * _sigmoid(yn)).astype(BF16)

    o, s_new = _gla_chunk(qk_scr[rows, 0:QK_W], qk_scr[rows, QK_W:2 * QK_W], v_scr[rows, :], la_scr[rows, :], s)
    for h in range(HEADS):
        hs = slice(h * DV, (h + 1) * DV)
        oh = o[:, hs]
        on = oh * lax.rsqrt(jnp.mean(oh * oh, axis=-1, keepdims=True) + EPS) * gng_ref[:, hs]
        m_ref[rows, C_CONV + h * DV:C_CONV + (h + 1) * DV] = (on * sg_scr[rows, hs]).astype(BF16)
    return s_new


def _ffn_inproj_tile(x_ref, n1_ref, wg_ref, wu_ref, wd_ref, n2_ref, win_ref, wgz_ref, w2_ref, gb_ref,
                     x1_ref, a_ref, qk_scr, v_scr, sg_scr, la_scr, store_z):
    x = x_ref[...]
    h = _rms(x, n1_ref[...]).astype(BF16)
    x1 = x + 0.5 * _swiglu(h, wg_ref, wu_ref, wd_ref, a_ref)
    x1_ref[...] = x1
    h2 = _rms(x1, n2_ref[...]).astype(BF16)
    ua = _dot(h2, win_ref[:, 0:C_CONV])
    ub = _dot(h2, win_ref[:, C_CONV:2 * C_CONV])
    store_z(ua * _sigmoid(ub))
    o = 2 * C_CONV
    qk_scr[:, 0:QK_W] = _dot(h2, win_ref[:, o:o + QK_W]) * (DK ** -0.5)
    qk_scr[:, QK_W:2 * QK_W] = _dot(h2, win_ref[:, o + QK_W:o + 2 * QK_W])
    o += 2 * QK_W
    v_scr[...] = _dot(h2, win_ref[:, o:o + V_W]).astype(BF16)
    g = _dot(h2, win_ref[:, o + V_W:o + 2 * V_W])
    sg_scr[...] = g * _sigmoid(g)
    gz = _dot(h2, wgz_ref[...])
    pre = jnp.dot(gz, w2_ref[...], preferred_element_type=F32, precision=lax.Precision.HIGHEST) + gb_ref[...]
    la_scr[...] = (jnp.minimum(pre, 0.0) - jnp.log(1.0 + jnp.exp(-jnp.abs(pre)))) * (1.0 / GATE_TAU)


def _zero_staging(zbuf, qk_scr, v_scr, sg_scr, la_scr):
    zbuf[...] = jnp.zeros(zbuf.shape, F32)
    qk_scr[...] = jnp.zeros(qk_scr.shape, F32)
    v_scr[...] = jnp.zeros(v_scr.shape, BF16)
    sg_scr[...] = jnp.zeros(sg_scr.shape, F32)
    la_scr[...] = jnp.zeros(la_scr.shape, F32)


def _ffn_mixer_prompt_kernel(x_ref, n1_ref, wg_ref, wu_ref, wd_ref, n2_ref, win_ref, wgz_ref, w2_ref, gb_ref,
                             dw_ref, dwb_ref, lng_ref, lnb_ref, gng_ref,
                             x1_ref, m_ref, cc_ref, so_ref,
                             a_ref, zbuf, qk_scr, v_scr, sg_scr, la_scr, s_scr, *, tiles_per_seq):
    i = pl.program_id(0)
    j = jnp.maximum(i - 1, 0)

    @pl.when(i == 0)
    def _():
        _zero_staging(zbuf, qk_scr, v_scr, sg_scr, la_scr)

    @pl.when(j % tiles_per_seq == 0)
    def _():
        zbuf[:, 0:HIST, :] = jnp.zeros((N_SLAB, HIST, LANES), F32)
        s_scr[...] = jnp.zeros(s_scr.shape, F32)

    s = s_scr[...]
    for c in range(TM // CHUNK):
        rows = slice(c * CHUNK, (c + 1) * CHUNK)
        s = _mix_chunk(HIST + c * CHUNK, rows, s, zbuf, qk_scr, v_scr, sg_scr, la_scr,
                       dw_ref, dwb_ref, lng_ref, lnb_ref, gng_ref, m_ref)
    s_scr[...] = s
    zbuf[:, 0:HIST, :] = zbuf[:, TM:TM + HIST, :]

    def store_z(z):
        for sl in range(N_SLAB):
            zbuf[sl, HIST:HIST + TM, :] = z[:, sl * LANES:(sl + 1) * LANES]

    _ffn_inproj_tile(x_ref, n1_ref, wg_ref, wu_ref, wd_ref, n2_ref, win_ref, wgz_ref, w2_ref, gb_ref,
                     x1_ref, a_ref, qk_scr, v_scr, sg_scr, la_scr, store_z)

    @pl.when(j % tiles_per_seq == tiles_per_seq - 1)
    def _():
        b = j // tiles_per_seq
        so_ref[b] = s_scr[...]
        for sl in range(N_SLAB):
            cc_ref[b, :, sl * LANES:(sl + 1) * LANES] = zbuf[sl, 0:HIST, :]


def _ffn_mixer_prompt(x, w, l, bsz, length):
    n_tiles = x.shape[0] // TM
    tiles_per_seq = length // TM
    row_in = pl.BlockSpec((TM, D_MODEL), lambda i: (jnp.minimum(i, n_tiles - 1), 0))
    row_out = pl.BlockSpec((TM, D_MODEL), lambda i: (jnp.minimum(i, n_tiles - 1), 0))
    mix_out = pl.BlockSpec((TM, D_MODEL), lambda i: (jnp.maximum(i - 1, 0), 0))
    return pl.pallas_call(
        functools.partial(_ffn_mixer_prompt_kernel, tiles_per_seq=tiles_per_seq),
        grid=(n_tiles + 1,),
        in_specs=[row_in] + [_VMEM] * 14,
        out_specs=[row_out, mix_out, _VMEM, _VMEM],
        out_shape=[jax.ShapeDtypeStruct((n_tiles * TM, D_MODEL), F32),
                   jax.ShapeDtypeStruct((n_tiles * TM, D_MODEL), BF16),
                   jax.ShapeDtypeStruct((bsz, HIST, C_CONV), F32),
                   jax.ShapeDtypeStruct((bsz, QK_W, DV), F32)],
        scratch_shapes=[pltpu.VMEM((TM, D_FF), BF16),
                        pltpu.VMEM((N_SLAB, HIST + TM, LANES), F32),
                        pltpu.VMEM((TM, 2 * QK_W), F32), pltpu.VMEM((TM, V_W), BF16),
                        pltpu.VMEM((TM, V_W), F32), pltpu.VMEM((TM, QK_W), F32),
                        pltpu.VMEM((QK_W, DV), F32)],
        compiler_params=pltpu.CompilerParams(dimension_semantics=("arbitrary",), vmem_limit_bytes=VMEM_LIMIT),
        name="ffn_mixer_prompt",
    )(x, w["n1"][l], w["wg1"][l], w["wu1"][l], w["wd1"][l], w["n2"][l], w["win"][l], w["wgz"][l], w["w2"][l],
      w["gb"][l], w["dw"][l], w["dwb"][l], w["lng"][l], w["lnb"][l], w["gng"][l])


def _ffn_mixer_sample_kernel(x_ref, n1_ref, wg_ref, wu_ref, wd_ref, n2_ref, win_ref, wgz_ref, w2_ref, gb_ref,
                             dw_ref, dwb_ref, lng_ref, lnb_ref, gng_ref, cci_ref, si_ref,
                             x1_ref, m_ref, cc_ref, so_ref,
                             a_ref, zbuf, qk_scr, v_scr, sg_scr, la_scr):
    i = pl.program_id(0)
    span = HIST + CHUNK

    @pl.when(i == 0)
    def _():
        _zero_staging(zbuf, qk_scr, v_scr, sg_scr, la_scr)

    for c in range(TM // CHUNK):
        rows = slice(c * CHUNK, (c + 1) * CHUNK)
        for sl in range(N_SLAB):
            zbuf[sl, c * span:c * span + HIST, :] = cci_ref[c, :, sl * LANES:(sl + 1) * LANES]
        so_ref[c] = _mix_chunk(c * span + HIST, rows, si_ref[c], zbuf, qk_scr, v_scr, sg_scr, la_scr,
                               dw_ref, dwb_ref, lng_ref, lnb_ref, gng_ref, m_ref)
        for sl in range(N_SLAB):
            cc_ref[c, :, sl * LANES:(sl + 1) * LANES] = zbuf[sl, c * span + CHUNK:(c + 1) * span, :]

    def store_z(z):
        for c in range(TM // CHUNK):
            for sl in range(N_SLAB):
                zbuf[sl, c * span + HIST:(c + 1) * span, :] = z[c * CHUNK:(c + 1) * CHUNK, sl * LANES:(sl + 1) * LANES]

    _ffn_inproj_tile(x_ref, n1_ref, wg_ref, wu_ref, wd_ref, n2_ref, win_ref, wgz_ref, w2_ref, gb_ref,
                     x1_ref, a_ref, qk_scr, v_scr, sg_scr, la_scr, store_z)


def _ffn_mixer_sample(x, cc, s0, w, l):
    n_tiles = x.shape[0] // TM
    nseq = TM // CHUNK
    row_in = pl.BlockSpec((TM, D_MODEL), lambda i: (jnp.minimum(i, n_tiles - 1), 0))
    mix_out = pl.BlockSpec((TM, D_MODEL), lambda i: (jnp.maximum(i - 1, 0), 0))
    per_seq = lambda r, c: pl.BlockSpec((nseq, r, c), lambda i: (jnp.maximum(i - 1, 0), 0, 0))
    return pl.pallas_call(
        _ffn_mixer_sample_kernel,
        grid=(n_tiles + 1,),
        in_specs=[row_in] + [_VMEM] * 14 + [per_seq(HIST, C_CONV), per_seq(QK_W, DV)],
        out_specs=[row_in, mix_out, per_seq(HIST, C_CONV), per_seq(QK_W, DV)],
        out_shape=[jax.ShapeDtypeStruct((n_tiles * TM, D_MODEL), F32),
                   jax.ShapeDtypeStruct((n_tiles * TM, D_MODEL), BF16),
                   jax.ShapeDtypeStruct((n_tiles * nseq, HIST, C_CONV), F32),
                   jax.ShapeDtypeStruct((n_tiles * nseq, QK_W, DV), F32)],
        scratch_shapes=[pltpu.VMEM((TM, D_FF), BF16),
                        pltpu.VMEM((N_SLAB, nseq * (HIST + CHUNK), LANES), F32),
                        pltpu.VMEM((TM, 2 * QK_W), F32), pltpu.VMEM((TM, V_W), BF16),
                        pltpu.VMEM((TM, V_W), F32), pltpu.VMEM((TM, QK_W), F32)],
        compiler_params=pltpu.CompilerParams(dimension_semantics=("arbitrary",), vmem_limit_bytes=VMEM_LIMIT),
        name="ffn_mixer_sample",
    )(x, w["n1"][l], w["wg1"][l], w["wu1"][l], w["wd1"][l], w["n2"][l], w["win"][l], w["wgz"][l], w["w2"][l],
      w["gb"][l], w["dw"][l], w["dwb"][l], w["lng"][l], w["lnb"][l], w["gng"][l], cc, s0)


def _outproj_ffn_kernel(x1_ref, m_ref, wo_ref, n3_ref, wg_ref, wu_ref, wd_ref, nf_ref, o_ref, a_ref, *, final):
    x2 = x1_ref[...] + _dot(m_ref[...], wo_ref[...])
    h = _rms(x2, n3_ref[...]).astype(BF16)
    x3 = x2 + 0.5 * _swiglu(h, wg_ref, wu_ref, wd_ref, a_ref)
    o_ref[...] = _rms(x3, nf_ref[...]) if final else x3


def _outproj_ffn(x1, mix, w, l, final):
    m = x1.shape[0]
    row = pl.BlockSpec((TM, D_MODEL), lambda i: (i, 0))
    return pl.pallas_call(
        functools.partial(_outproj_ffn_kernel, final=final),
        grid=(m // TM,),
        in_specs=[row, row] + [_VMEM] * 6,
        out_specs=row,
        out_shape=jax.ShapeDtypeStruct((m, D_MODEL), F32),
        scratch_shapes=[pltpu.VMEM((TM, D_FF), BF16)],
        compiler_params=pltpu.CompilerParams(dimension_semantics=("arbitrary",), vmem_limit_bytes=VMEM_LIMIT),
        name="outproj_ffn",
    )(x1, mix, w["wo"][l], w["n3"][l], w["wg2"][l], w["wu2"][l], w["wd2"][l], w["nf"])


def _trunk(x, cache, state, w):
    bsz, length, _ = x.shape
    xf = x.reshape(bsz * length, D_MODEL)
    new_cc, new_s = [], []
    for l in range(DEPTH):
        if cache is None:
            x1, mix, cc_l, s_l = _ffn_mixer_prompt(xf, w, l, bsz, length)
        else:
            cc_in = jnp.pad(cache[l], ((0, 0), (HIST - (CONV_W - 1), 0), (0, 0)))
            x1, mix, cc_l, s_l = _ffn_mixer_sample(xf, cc_in, state[l].reshape(bsz, QK_W, DV), w, l)
        new_cc.append(cc_l[:, HIST - (CONV_W - 1):, :])
        new_s.append(s_l.reshape(bsz, HEADS, DK, DV))
        xf = _outproj_ffn(x1, mix, w, l, final=(l == DEPTH - 1))
    return xf.reshape(bsz, length, D_MODEL), jnp.stack(new_cc), jnp.stack(new_s)


def kernel(x_prompt, x_sample, cache_conv, state_gla, ffn1_norm, ffn1_w_gate, ffn1_w_up, ffn1_w_down, mix_norm, w_in, conv_dw_w, conv_dw_b, conv_ln_g, conv_ln_b, gla_gate_w2, gla_gate_b, gla_norm_g, w_out, ffn2_norm, ffn2_w_gate, ffn2_w_up, ffn2_w_down, final_norm):
    assert x_prompt.shape[1] % TM == 0 and x_sample.shape[1] == CHUNK and (x_sample.shape[0] * CHUNK) % TM == 0
    row = lambda a: a.reshape(DEPTH, 1, a.shape[-1])
    n_main = 2 * C_CONV + 2 * QK_W + 2 * V_W
    w = dict(
        n1=row(ffn1_norm), wg1=ffn1_w_gate.astype(BF16), wu1=ffn1_w_up.astype(BF16), wd1=ffn1_w_down.astype(BF16),
        n2=row(mix_norm), win=w_in[:, :, :n_main].astype(BF16),
        wgz=jnp.pad(w_in[:, :, n_main:], ((0, 0), (0, 0), (0, LANES - GATE_RANK))).astype(BF16),
        w2=jnp.pad(gla_gate_w2, ((0, 0), (0, LANES - GATE_RANK), (0, 0))), gb=row(gla_gate_b),
        dw=jnp.pad(conv_dw_w, ((0, 0), (0, 1), (0, 0))), dwb=row(conv_dw_b), lng=row(conv_ln_g), lnb=row(conv_ln_b),
        gng=row(gla_norm_g), wo=w_out.astype(BF16), n3=row(ffn2_norm), wg2=ffn2_w_gate.astype(BF16),
        wu2=ffn2_w_up.astype(BF16), wd2=ffn2_w_down.astype(BF16), nf=final_norm.reshape(1, D_MODEL),
    )
    y_p, cc_p, s_p = _trunk(x_prompt, None, None, w)
    y_s, cc_s, s_s = _trunk(x_sample, cache_conv, state_gla, w)
    return (y_p, y_s, cc_p, s_p, cc_s, s_s)
```

```python
import functools

import jax
import jax.numpy as jnp
from jax import lax
from jax.experimental import pallas as pl
from jax.experimental.pallas import tpu as pltpu

D_MODEL = 1024
DEPTH = 2
C_CONV = 512
CONV_W = 31
HEADS = 4
DK = 64
DV = 128
QK_W = HEADS * DK
V_W = HEADS * DV
GATE_RANK = 16
GATE_TAU = 16.0
D_FF = 2816
EPS = 1e-6
CHUNK = 64
SUB = 16
NSUB = CHUNK // SUB

LANES = 128
N_SLAB = C_CONV // LANES
HIST = 32
FF_CHUNK = 256
FF_GROUPS = ((0, 768), (768, 1536), (1536, 2304), (2304, D_FF))
TM = 512
VMEM_LIMIT = 58 * 1024 * 1024

F32 = jnp.float32
BF16 = jnp.bfloat16


def _sigmoid(x):
    return 1.0 / (1.0 + jnp.exp(-x))


def _rms(x, g):
    return x * lax.rsqrt(jnp.mean(x * x, axis=-1, keepdims=True) + EPS) * g


def _dot(a, b):
    return jnp.dot(a, b, preferred_element_type=F32)


def _gate_up(h, wg_ref, wu_ref, a_ref, lo, hi, between=None):
    for c in range(lo, hi, FF_CHUNK):
        if between is not None:
            next(between, None)
        sl = slice(c, c + FF_CHUNK)
        g = _dot(h, wg_ref[:, sl])
        u = _dot(h, wu_ref[:, sl])
        a_ref[:, sl] = (g * _sigmoid(g) * u).astype(BF16)


def _swiglu(h, wg_ref, wu_ref, wd_ref, a_ref):
    _gate_up(h, wg_ref, wu_ref, a_ref, 0, D_FF)
    return _dot(a_ref[...], wd_ref[...])


_VMEM = pl.BlockSpec(memory_space=pltpu.VMEM)


def _gla_cumsum(la):
    r = lax.broadcasted_iota(jnp.int32, (CHUNK, CHUNK), 0)
    c = lax.broadcasted_iota(jnp.int32, (CHUNK, CHUNK), 1)
    tril = (r >= c).astype(BF16)
    la_hi = la.astype(BF16)
    r1 = la - la_hi.astype(F32)
    la_mid = r1.astype(BF16)
    la_lo = (r1 - la_mid.astype(F32)).astype(BF16)
    return _dot(tril, la_hi) + _dot(tril, la_mid) + _dot(tril, la_lo)


def _gla_operands(q, k, b):
    lane_head = lax.broadcasted_iota(jnp.int32, (CHUNK, QK_W), 1) // DK
    row_head = lax.broadcasted_iota(jnp.int32, (CHUNK, QK_W), 0) // SUB
    head_mask = lane_head == row_head
    j_idx = lax.broadcasted_iota(jnp.int32, (CHUNK, QK_W), 0)
    b_last = b[CHUNK - 1:CHUNK, :]
    qe = q * jnp.exp(b)
    qxs, kxs, qe_blocks = [], [], []
    for i in range(NSUB):
        rows = slice(i * SUB, (i + 1) * SUB)
        b_ref = b[i * SUB - 1:i * SUB, :] if i > 0 else jnp.zeros((1, QK_W), F32)
        qt = q[rows] * jnp.exp(b[rows] - b_ref)
        qxs.append(jnp.where(head_mask, jnp.concatenate([qt] * HEADS, axis=0), 0.0).astype(BF16))
        kxs.append(jnp.where(j_idx < (i + 1) * SUB, k * jnp.exp(b_ref - b), 0.0).astype(BF16))
        qe_blocks.append(jnp.where(head_mask, jnp.concatenate([qe[rows]] * HEADS, axis=0), 0.0).astype(BF16))
    qe_all = jnp.concatenate(qe_blocks, axis=0)
    kd = (k * jnp.exp(b_last - b)).astype(BF16)
    decay = jnp.exp(jnp.broadcast_to(b_last, (DV, QK_W)).T)
    return qxs, kxs, qe_all, kd, decay


def _gla_scores(qxs, kxs):
    r = lax.broadcasted_iota(jnp.int32, (CHUNK, CHUNK), 0)
    c = lax.broadcasted_iota(jnp.int32, (CHUNK, CHUNK), 1)
    row_local = r % SUB
    p_blocks = []
    for i in range(NSUB):
        a = lax.dot_general(qxs[i], kxs[i], (((1,), (1,)), ((), ())), preferred_element_type=F32)
        p_blocks.append(jnp.where(c <= i * SUB + row_local, a, 0.0).astype(BF16))
    return jnp.concatenate(p_blocks, axis=0)


def _gla_update(kd, v):
    upd = lax.dot_general(kd, v, (((0,), (0,)), ((), ())), preferred_element_type=F32)
    return jnp.concatenate([upd[h * DK:(h + 1) * DK, h * DV:(h + 1) * DV] for h in range(HEADS)], axis=0)


def _gla_output(p_all, qe_all, v, s):
    intra = _dot(p_all, v)
    inter = _dot(qe_all, s.astype(BF16))
    o_rows = []
    for i in range(NSUB):
        blocks = []
        for h in range(HEADS):
            r0 = i * CHUNK + h * SUB
            blocks.append(intra[r0:r0 + SUB, h * DV:(h + 1) * DV] + inter[r0:r0 + SUB, :])
        o_rows.append(jnp.concatenate(blocks, axis=1))
    return jnp.concatenate(o_rows, axis=0)


def _gla_tile(n_chunk, init_state, chained, store_state, qk_scr, v_scr, sg_scr, la_scr, gng_ref, m_ref):
    rows = [slice(c * CHUNK, (c + 1) * CHUNK) for c in range(n_chunk)]
    bs = [_gla_cumsum(la_scr[r, :]) for r in rows]
    yield
    ops = [_gla_operands(qk_scr[r, 0:QK_W], qk_scr[r, QK_W:2 * QK_W], b) for r, b in zip(rows, bs)]
    ps = [_gla_scores(o[0], o[1]) for o in ops]
    upds = [_gla_update(o[3], v_scr[r, :]) for r, o in zip(rows, ops)]
    yield
    s_in, s_prev = [], None
    for c in range(n_chunk):
        s_in.append(s_prev if chained and c > 0 else init_state(c))
        s_prev = ops[c][4] * s_in[c] + upds[c]
        store_state(c, s_prev)
    outs = [_gla_output(ps[c], ops[c][2], v_scr[rows[c], :], s_in[c]) for c in range(n_chunk)]
    yield
    for c in range(n_chunk):
        for h in range(HEADS):
            hs = slice(h * DV, (h + 1) * DV)
            oh = outs[c][:, hs]
            on = oh * lax.rsqrt(jnp.mean(oh * oh, axis=-1, keepdims=True) + EPS) * gng_ref[:, hs]
            m_ref[rows[c], C_CONV + h * DV:C_CONV + (h + 1) * DV] = (on * sg_scr[rows[c], hs]).astype(BF16)
    yield


def _conv_chunk(zrow, rows, zbuf, dw_ref, dwb_ref, lng_ref, lnb_ref, m_ref):
    acc = []
    for j in range(N_SLAB):
        ls = slice(j * LANES, (j + 1) * LANES)
        a = jnp.broadcast_to(dwb_ref[:, ls], (CHUNK, LANES))
        for w in range(CONV_W):
            a = a + zbuf[j, pl.ds(zrow - (CONV_W - 1) + w, CHUNK, stride=1), :] * dw_ref[w:w + 1, ls]
        acc.append(a)
    mu = sum(jnp.sum(a, axis=-1, keepdims=True) for a in acc) * (1.0 / C_CONV)
    xc = [a - mu for a in acc]
    rstd = lax.rsqrt(sum(jnp.sum(x * x, axis=-1, keepdims=True) for x in xc) * (1.0 / C_CONV) + EPS)
    for j in range(N_SLAB):
        ls = slice(j * LANES, (j + 1) * LANES)
        yn = xc[j] * rstd * lng_ref[:, ls] + lnb_ref[:, ls]
        m_ref[rows, ls] = (yn * _sigmoid(yn)).astype(BF16)


def _ffn_inproj_tile(x_ref, n1_ref, wg_ref, wu_ref, wd_ref, n2_ref, win_ref, wgz_ref, w2_ref, gb_ref,
                     x1_ref, a_ref, h_ref, qk_scr, v_scr, sg_scr, la_scr, store_z, conv_work, gla_work):
    h_ref[...] = _rms(x_ref[...], n1_ref[...]).astype(BF16)
    for gi, (lo, hi) in enumerate(FF_GROUPS):
        @pl.when(pl.program_id(0) >= -gi)
        def _():
            gen = gla_work() if gi == 0 else None
            _gate_up(h_ref[...], wg_ref, wu_ref, a_ref, lo, hi, gen)
            if gen is not None:
                for _ in gen:
                    pass
            conv_work[gi]()
    x1 = x_ref[...] + 0.5 * _dot(a_ref[...], wd_ref[...])
    x1_ref[...] = x1
    h2 = _rms(x1, n2_ref[...]).astype(BF16)
    ua = _dot(h2, win_ref[:, 0:C_CONV])
    ub = _dot(h2, win_ref[:, C_CONV:2 * C_CONV])
    store_z(ua * _sigmoid(ub))
    o = 2 * C_CONV
    qk_scr[:, 0:QK_W] = _dot(h2, win_ref[:, o:o + QK_W]) * (DK ** -0.5)
    qk_scr[:, QK_W:2 * QK_W] = _dot(h2, win_ref[:, o + QK_W:o + 2 * QK_W])
    o += 2 * QK_W
    v_scr[...] = _dot(h2, win_ref[:, o:o + V_W]).astype(BF16)
    g = _dot(h2, win_ref[:, o + V_W:o + 2 * V_W])
    sg_scr[...] = g * _sigmoid(g)
    gz = _dot(h2, wgz_ref[...])
    pre = jnp.dot(gz, w2_ref[...], preferred_element_type=F32, precision=lax.Precision.HIGHEST) + gb_ref[...]
    la_scr[...] = (jnp.minimum(pre, 0.0) - jnp.log(1.0 + jnp.exp(-jnp.abs(pre)))) * (1.0 / GATE_TAU)


def _zero_staging(zbuf, qk_scr, v_scr, sg_scr, la_scr):
    zbuf[...] = jnp.zeros(zbuf.shape, F32)
    qk_scr[...] = jnp.zeros(qk_scr.shape, F32)
    v_scr[...] = jnp.zeros(v_scr.shape, BF16)
    sg_scr[...] = jnp.zeros(sg_scr.shape, F32)
    la_scr[...] = jnp.zeros(la_scr.shape, F32)


def _ffn_mixer_prompt_kernel(x_ref, n1_ref, wg_ref, wu_ref, wd_ref, n2_ref, win_ref, wgz_ref, w2_ref, gb_ref,
                             dw_ref, dwb_ref, lng_ref, lnb_ref, gng_ref,
                             x1_ref, m_ref, cc_ref, so_ref,
                             a_ref, h_ref, zbuf, qk_scr, v_scr, sg_scr, la_scr, s_scr, *, tiles_per_seq):
    i = pl.program_id(0)
    j = jnp.maximum(i - 1, 0)

    @pl.when(i == 0)
    def _():
        _zero_staging(zbuf, qk_scr, v_scr, sg_scr, la_scr)

    @pl.when(j % tiles_per_seq == 0)
    def _():
        zbuf[:, 0:HIST, :] = jnp.zeros((N_SLAB, HIST, LANES), F32)
        s_scr[...] = jnp.zeros(s_scr.shape, F32)

    n_chunk = TM // CHUNK
    per_group = n_chunk // len(FF_GROUPS)

    def conv_group(gi):
        def run():
            for c in range(gi * per_group, (gi + 1) * per_group):
                _conv_chunk(HIST + c * CHUNK, slice(c * CHUNK, (c + 1) * CHUNK), zbuf,
                            dw_ref, dwb_ref, lng_ref, lnb_ref, m_ref)
        return run

    def store_state(c, s):
        if c == n_chunk - 1:
            s_scr[...] = s

    def gla_work():
        return _gla_tile(n_chunk, lambda c: s_scr[...], True, store_state,
                         qk_scr, v_scr, sg_scr, la_scr, gng_ref, m_ref)

    def store_z(z):
        zbuf[:, 0:HIST, :] = zbuf[:, TM:TM + HIST, :]
        for sl in range(N_SLAB):
            zbuf[sl, HIST:HIST + TM, :] = z[:, sl * LANES:(sl + 1) * LANES]

    _ffn_inproj_tile(x_ref, n1_ref, wg_ref, wu_ref, wd_ref, n2_ref, win_ref, wgz_ref, w2_ref, gb_ref,
                     x1_ref, a_ref, h_ref, qk_scr, v_scr, sg_scr, la_scr, store_z,
                     [conv_group(g) for g in range(len(FF_GROUPS))], gla_work)

    @pl.when(j % tiles_per_seq == tiles_per_seq - 1)
    def _():
        b = j // tiles_per_seq
        so_ref[b] = s_scr[...]
        for sl in range(N_SLAB):
            cc_ref[b, :, sl * LANES:(sl + 1) * LANES] = zbuf[sl, 0:HIST, :]


def _ffn_mixer_prompt(x, w, l, bsz, length):
    n_tiles = x.shape[0] // TM
    tiles_per_seq = length // TM
    row_in = pl.BlockSpec((TM, D_MODEL), lambda i: (jnp.minimum(i, n_tiles - 1), 0))
    row_out = pl.BlockSpec((TM, D_MODEL), lambda i: (jnp.minimum(i, n_tiles - 1), 0))
    mix_out = pl.BlockSpec((TM, D_MODEL), lambda i: (jnp.maximum(i - 1, 0), 0))
    return pl.pallas_call(
        functools.partial(_ffn_mixer_prompt_kernel, tiles_per_seq=tiles_per_seq),
        grid=(n_tiles + 1,),
        in_specs=[row_in] + [_VMEM] * 14,
        out_specs=[row_out, mix_out, _VMEM, _VMEM],
        out_shape=[jax.ShapeDtypeStruct((n_tiles * TM, D_MODEL), F32),
                   jax.ShapeDtypeStruct((n_tiles * TM, D_MODEL), BF16),
                   jax.ShapeDtypeStruct((bsz, HIST, C_CONV), F32),
                   jax.ShapeDtypeStruct((bsz, QK_W, DV), F32)],
        scratch_shapes=[pltpu.VMEM((TM, D_FF), BF16), pltpu.VMEM((TM, D_MODEL), BF16),
                        pltpu.VMEM((N_SLAB, HIST + TM, LANES), F32),
                        pltpu.VMEM((TM, 2 * QK_W), F32), pltpu.VMEM((TM, V_W), BF16),
                        pltpu.VMEM((TM, V_W), F32), pltpu.VMEM((TM, QK_W), F32),
                        pltpu.VMEM((QK_W, DV), F32)],
        compiler_params=pltpu.CompilerParams(dimension_semantics=("arbitrary",), vmem_limit_bytes=VMEM_LIMIT),
        name="ffn_mixer_prompt",
    )(x, w["n1"][l], w["wg1"][l], w["wu1"][l], w["wd1"][l], w["n2"][l], w["win"][l], w["wgz"][l], w["w2"][l],
      w["gb"][l], w["dw"][l], w["dwb"][l], w["lng"][l], w["lnb"][l], w["gng"][l])


def _ffn_mixer_sample_kernel(x_ref, n1_ref, wg_ref, wu_ref, wd_ref, n2_ref, win_ref, wgz_ref, w2_ref, gb_ref,
                             dw_ref, dwb_ref, lng_ref, lnb_ref, gng_ref, cci_ref, si_ref,
                             x1_ref, m_ref, cc_ref, so_ref,
                             a_ref, h_ref, zbuf, qk_scr, v_scr, sg_scr, la_scr):
    i = pl.program_id(0)
    span = HIST + CHUNK

    @pl.when(i == 0)
    def _():
        _zero_staging(zbuf, qk_scr, v_scr, sg_scr, la_scr)

    n_chunk = TM // CHUNK
    per_group = n_chunk // len(FF_GROUPS)

    def conv_group(gi):
        def run():
            for c in range(gi * per_group, (gi + 1) * per_group):
                for sl in range(N_SLAB):
                    zbuf[sl, c * span:c * span + HIST, :] = cci_ref[c, :, sl * LANES:(sl + 1) * LANES]
                _conv_chunk(c * span + HIST, slice(c * CHUNK, (c + 1) * CHUNK), zbuf,
                            dw_ref, dwb_ref, lng_ref, lnb_ref, m_ref)
                for sl in range(N_SLAB):
                    cc_ref[c, :, sl * LANES:(sl + 1) * LANES] = zbuf[sl, c * span + CHUNK:(c + 1) * span, :]
        return run

    def store_state(c, s):
        so_ref[c] = s

    def gla_work():
        return _gla_tile(n_chunk, lambda c: si_ref[c], False, store_state,
                         qk_scr, v_scr, sg_scr, la_scr, gng_ref, m_ref)

    def store_z(z):
        for c in range(n_chunk):
            for sl in range(N_SLAB):
                zbuf[sl, c * span + HIST:(c + 1) * span, :] = z[c * CHUNK:(c + 1) * CHUNK, sl * LANES:(sl + 1) * LANES]

    _ffn_inproj_tile(x_ref, n1_ref, wg_ref, wu_ref, wd_ref, n2_ref, win_ref, wgz_ref, w2_ref, gb_ref,
                     x1_ref, a_ref, h_ref, qk_scr, v_scr, sg_scr, la_scr, store_z,
                     [conv_group(g) for g in range(len(FF_GROUPS))], gla_work)


def _ffn_mixer_sample(x, cc, s0, w, l):
    n_tiles = x.shape[0] // TM
    nseq = TM // CHUNK
    row_in = pl.BlockSpec((TM, D_MODEL), lambda i: (jnp.minimum(i, n_tiles - 1), 0))
    mix_out = pl.BlockSpec((TM, D_MODEL), lambda i: (jnp.maximum(i - 1, 0), 0))
    per_seq = lambda r, c: pl.BlockSpec((nseq, r, c), lambda i: (jnp.maximum(i - 1, 0), 0, 0))
    return pl.pallas_call(
        _ffn_mixer_sample_kernel,
        grid=(n_tiles + 1,),
        in_specs=[row_in] + [_VMEM] * 14 + [per_seq(HIST, C_CONV), per_seq(QK_W, DV)],
        out_specs=[row_in, mix_out, per_seq(HIST, C_CONV), per_seq(QK_W, DV)],
        out_shape=[jax.ShapeDtypeStruct((n_tiles * TM, D_MODEL), F32),
                   jax.ShapeDtypeStruct((n_tiles * TM, D_MODEL), BF16),
                   jax.ShapeDtypeStruct((n_tiles * nseq, HIST, C_CONV), F32),
                   jax.ShapeDtypeStruct((n_tiles * nseq, QK_W, DV), F32)],
        scratch_shapes=[pltpu.VMEM((TM, D_FF), BF16), pltpu.VMEM((TM, D_MODEL), BF16),
                        pltpu.VMEM((N_SLAB, nseq * (HIST + CHUNK), LANES), F32),
                        pltpu.VMEM((TM, 2 * QK_W), F32), pltpu.VMEM((TM, V_W), BF16),
                        pltpu.VMEM((TM, V_W), F32), pltpu.VMEM((TM, QK_W), F32)],
        compiler_params=pltpu.CompilerParams(dimension_semantics=("arbitrary",), vmem_limit_bytes=VMEM_LIMIT),
        name="ffn_mixer_sample",
    )(x, w["n1"][l], w["wg1"][l], w["wu1"][l], w["wd1"][l], w["n2"][l], w["win"][l], w["wgz"][l], w["w2"][l],
      w["gb"][l], w["dw"][l], w["dwb"][l], w["lng"][l], w["lnb"][l], w["gng"][l], cc, s0)


def _outproj_ffn_kernel(x1_ref, m_ref, wo_ref, n3_ref, wg_ref, wu_ref, wd_ref, nf_ref, o_ref, a_ref, *, final):
    x2 = x1_ref[...] + _dot(m_ref[...], wo_ref[...])
    h = _rms(x2, n3_ref[...]).astype(BF16)
    x3 = x2 + 0.5 * _swiglu(h, wg_ref, wu_ref, wd_ref, a_ref)
    o_ref[...] = _rms(x3, nf_ref[...]) if final else x3


def _outproj_ffn(x1, mix, w, l, final):
    m = x1.shape[0]
    row = pl.BlockSpec((TM, D_MODEL), lambda i: (i, 0))
    return pl.pallas_call(
        functools.partial(_outproj_ffn_kernel, final=final),
        grid=(m // TM,),
        in_specs=[row, row] + [_VMEM] * 6,
        out_specs=row,
        out_shape=jax.ShapeDtypeStruct((m, D_MODEL), F32),
        scratch_shapes=[pltpu.VMEM((TM, D_FF), BF16)],
        compiler_params=pltpu.CompilerParams(dimension_semantics=("arbitrary",), vmem_limit_bytes=VMEM_LIMIT),
        name="outproj_ffn",
    )(x1, mix, w["wo"][l], w["n3"][l], w["wg2"][l], w["wu2"][l], w["wd2"][l], w["nf"])


def _trunk(x, cache, state, w):
    bsz, length, _ = x.shape
    xf = x.reshape(bsz * length, D_MODEL)
    new_cc, new_s = [], []
    for l in range(DEPTH):
        if cache is None:
            x1, mix, cc_l, s_l = _ffn_mixer_prompt(xf, w, l, bsz, length)
        else:
            cc_in = jnp.pad(cache[l], ((0, 0), (HIST - (CONV_W - 1), 0), (0, 0)))
            x1, mix, cc_l, s_l = _ffn_mixer_sample(xf, cc_in, state[l].reshape(bsz, QK_W, DV), w, l)
        new_cc.append(cc_l[:, HIST - (CONV_W - 1):, :])
        new_s.append(s_l.reshape(bsz, HEADS, DK, DV))
        xf = _outproj_ffn(x1, mix, w, l, final=(l == DEPTH - 1))
    return xf.reshape(bsz, length, D_MODEL), jnp.stack(new_cc), jnp.stack(new_s)


def kernel(x_prompt, x_sample, cache_conv, state_gla, ffn1_norm, ffn1_w_gate, ffn1_w_up, ffn1_w_down, mix_norm, w_in, conv_dw_w, conv_dw_b, conv_ln_g, conv_ln_b, gla_gate_w2, gla_gate_b, gla_norm_g, w_out, ffn2_norm, ffn2_w_gate, ffn2_w_up, ffn2_w_down, final_norm):
    assert x_prompt.shape[1] % TM == 0 and x_sample.shape[1] == CHUNK and (x_sample.shape[0] * CHUNK) % TM == 0
    n_main = 2 * C_CONV + 2 * QK_W + 2 * V_W
    layers = lambda f: [f(l) for l in range(DEPTH)]
    bf = lambda a: layers(lambda l: a[l].astype(BF16))
    row = lambda a: layers(lambda l: a[l].reshape(1, a.shape[-1]))
    w = dict(
        n1=row(ffn1_norm), wg1=bf(ffn1_w_gate), wu1=bf(ffn1_w_up), wd1=bf(ffn1_w_down),
        n2=row(mix_norm), win=layers(lambda l: w_in[l, :, :n_main].astype(BF16)),
        wgz=layers(lambda l: jnp.pad(w_in[l, :, n_main:], ((0, 0), (0, LANES - GATE_RANK))).astype(BF16)),
        w2=layers(lambda l: jnp.pad(gla_gate_w2[l], ((0, LANES - GATE_RANK), (0, 0)))), gb=row(gla_gate_b),
        dw=layers(lambda l: jnp.pad(conv_dw_w[l], ((0, 1), (0, 0)))), dwb=row(conv_dw_b), lng=row(conv_ln_g),
        lnb=row(conv_ln_b), gng=row(gla_norm_g), wo=bf(w_out), n3=row(ffn2_norm), wg2=bf(ffn2_w_gate),
        wu2=bf(ffn2_w_up), wd2=bf(ffn2_w_down), nf=final_norm.reshape(1, D_MODEL),
    )
    y_p, cc_p, s_p = _trunk(x_prompt, None, None, w)
    y_s, cc_s, s_s = _trunk(x_sample, cache_conv, state_gla, w)
    return (y_p, y_s, cc_p, s_p, cc_s, s_s)
```

```python
import functools

import jax
import jax.numpy as jnp
from jax import lax
from jax.experimental import pallas as pl
from jax.experimental.pallas import tpu as pltpu

D_MODEL = 1024
DEPTH = 2
C_CONV = 512
CONV_W = 31
HEADS = 4
DK = 64
DV = 128
QK_W = HEADS * DK
V_W = HEADS * DV
N_MAIN = 2 * C_CONV + 2 * QK_W + 2 * V_W
GATE_RANK = 16
GATE_TAU = 16.0
D_FF = 2816
EPS = 1e-6
CHUNK = 64
SUB = 16
NSUB = CHUNK // SUB

LANES = 128
N_SLAB = C_CONV // LANES
HIST = 32
SPAN = HIST + CHUNK
FF_CHUNK = 256
FF_GROUPS = ((0, 768), (768, 1536), (1536, 2304), (2304, D_FF))
TM = 512
N_CHUNK = TM // CHUNK
W_ROWS = 128
VMEM_LIMIT = 58 * 1024 * 1024

F32 = jnp.float32
BF16 = jnp.bfloat16

_VMEM = pl.BlockSpec(memory_space=pltpu.VMEM)
_HBM = pl.BlockSpec(memory_space=pl.ANY)


def _sigmoid(x):
    return 1.0 / (1.0 + jnp.exp(-x))


def _rms(x, g):
    return x * lax.rsqrt(jnp.mean(x * x, axis=-1, keepdims=True) + EPS) * g


def _dot(a, b):
    return jnp.dot(a, b, preferred_element_type=F32)


def _load_cast(src, dst, stage, sem, n_rows, n_cols):
    n = n_rows // W_ROWS

    def copy(k):
        return pltpu.make_async_copy(src.at[pl.ds(k * W_ROWS, W_ROWS), pl.ds(0, n_cols)],
                                     stage.at[k % 2, :, pl.ds(0, n_cols)], sem.at[k % 2])

    copy(0).start()
    for k in range(n):
        copy(k).wait()
        if k + 1 < n:
            copy(k + 1).start()
        dst[k * W_ROWS:(k + 1) * W_ROWS, :] = stage[k % 2, :, 0:n_cols].astype(BF16)


def _gate_up(h, wg_ref, wu_ref, a_ref, lo, hi, between=None):
    for c in range(lo, hi, FF_CHUNK):
        if between is not None:
            next(between, None)
        sl = slice(c, c + FF_CHUNK)
        g = _dot(h, wg_ref[:, sl])
        u = _dot(h, wu_ref[:, sl])
        a_ref[:, sl] = (g * _sigmoid(g) * u).astype(BF16)


def _gla_cumsum(la):
    r = lax.broadcasted_iota(jnp.int32, (CHUNK, CHUNK), 0)
    c = lax.broadcasted_iota(jnp.int32, (CHUNK, CHUNK), 1)
    tril = (r >= c).astype(BF16)
    la_hi = la.astype(BF16)
    r1 = la - la_hi.astype(F32)
    la_mid = r1.astype(BF16)
    la_lo = (r1 - la_mid.astype(F32)).astype(BF16)
    return _dot(tril, la_hi) + _dot(tril, la_mid) + _dot(tril, la_lo)


def _gla_operands(q, k, b):
    lane_head = lax.broadcasted_iota(jnp.int32, (CHUNK, QK_W), 1) // DK
    row_head = lax.broadcasted_iota(jnp.int32, (CHUNK, QK_W), 0) // SUB
    head_mask = lane_head == row_head
    j_idx = lax.broadcasted_iota(jnp.int32, (CHUNK, QK_W), 0)
    b_last = b[CHUNK - 1:CHUNK, :]
    qe = q * jnp.exp(b)
    qxs, kxs, qe_blocks = [], [], []
    for i in range(NSUB):
        rows = slice(i * SUB, (i + 1) * SUB)
        b_ref = b[i * SUB - 1:i * SUB, :] if i > 0 else jnp.zeros((1, QK_W), F32)
        qt = q[rows] * jnp.exp(b[rows] - b_ref)
        qxs.append(jnp.where(head_mask, jnp.concatenate([qt] * HEADS, axis=0), 0.0).astype(BF16))
        kxs.append(jnp.where(j_idx < (i + 1) * SUB, k * jnp.exp(b_ref - b), 0.0).astype(BF16))
        qe_blocks.append(jnp.where(head_mask, jnp.concatenate([qe[rows]] * HEADS, axis=0), 0.0).astype(BF16))
    qe_all = jnp.concatenate(qe_blocks, axis=0)
    kd = (k * jnp.exp(b_last - b)).astype(BF16)
    decay = jnp.exp(jnp.broadcast_to(b_last, (DV, QK_W)).T)
    return qxs, kxs, qe_all, kd, decay


def _gla_scores(qxs, kxs):
    r = lax.broadcasted_iota(jnp.int32, (CHUNK, CHUNK), 0)
    c = lax.broadcasted_iota(jnp.int32, (CHUNK, CHUNK), 1)
    row_local = r % SUB
    p_blocks = []
    for i in range(NSUB):
        a = lax.dot_general(qxs[i], kxs[i], (((1,), (1,)), ((), ())), preferred_element_type=F32)
        p_blocks.append(jnp.where(c <= i * SUB + row_local, a, 0.0).astype(BF16))
    return jnp.concatenate(p_blocks, axis=0)


def _gla_update(kd, v):
    upd = lax.dot_general(kd, v, (((0,), (0,)), ((), ())), preferred_element_type=F32)
    return jnp.concatenate([upd[h * DK:(h + 1) * DK, h * DV:(h + 1) * DV] for h in range(HEADS)], axis=0)


def _gla_output(p_all, qe_all, v, s):
    intra = _dot(p_all, v)
    inter = _dot(qe_all, s.astype(BF16))
    o_rows = []
    for i in range(NSUB):
        blocks = []
        for h in range(HEADS):
            r0 = i * CHUNK + h * SUB
            blocks.append(intra[r0:r0 + SUB, h * DV:(h + 1) * DV] + inter[r0:r0 + SUB, :])
        o_rows.append(jnp.concatenate(blocks, axis=1))
    return jnp.concatenate(o_rows, axis=0)


def _gla_tile(state_in, store_state, qk_scr, v_scr, sg_scr, la_scr, gng_ref, m_ref):
    rows = [slice(c * CHUNK, (c + 1) * CHUNK) for c in range(N_CHUNK)]
    bs = [_gla_cumsum(la_scr[r, :]) for r in rows]
    yield
    ops = [_gla_operands(qk_scr[r, 0:QK_W], qk_scr[r, QK_W:2 * QK_W], b) for r, b in zip(rows, bs)]
    ps = [_gla_scores(o[0], o[1]) for o in ops]
    upds = [_gla_update(o[3], v_scr[r, :]) for r, o in zip(rows, ops)]
    yield
    s_in, s_prev = [], None
    for c in range(N_CHUNK):
        s_in.append(state_in(c, s_prev))
        s_prev = ops[c][4] * s_in[c] + upds[c]
        store_state(c, s_prev)
    outs = [_gla_output(ps[c], ops[c][2], v_scr[rows[c], :], s_in[c]) for c in range(N_CHUNK)]
    yield
    for c in range(N_CHUNK):
        for h in range(HEADS):
            hs = slice(h * DV, (h + 1) * DV)
            oh = outs[c][:, hs]
            on = oh * lax.rsqrt(jnp.mean(oh * oh, axis=-1, keepdims=True) + EPS) * gng_ref[:, hs]
            m_ref[rows[c], C_CONV + h * DV:C_CONV + (h + 1) * DV] = (on * sg_scr[rows[c], hs]).astype(BF16)
    yield


def _conv_chunk(zrow, rows, zbuf, dw_ref, dwb_ref, lng_ref, lnb_ref, m_ref):
    acc = []
    for j in range(N_SLAB):
        ls = slice(j * LANES, (j + 1) * LANES)
        a = jnp.broadcast_to(dwb_ref[:, ls], (CHUNK, LANES))
        for w in range(CONV_W):
            a = a + zbuf[j, pl.ds(zrow - (CONV_W - 1) + w, CHUNK, stride=1), :] * dw_ref[w:w + 1, ls]
        acc.append(a)
    mu = sum(jnp.sum(a, axis=-1, keepdims=True) for a in acc) * (1.0 / C_CONV)
    xc = [a - mu for a in acc]
    rstd = lax.rsqrt(sum(jnp.sum(x * x, axis=-1, keepdims=True) for x in xc) * (1.0 / C_CONV) + EPS)
    for j in range(N_SLAB):
        ls = slice(j * LANES, (j + 1) * LANES)
        yn = xc[j] * rstd * lng_ref[:, ls] + lnb_ref[:, ls]
        m_ref[rows, ls] = (yn * _sigmoid(yn)).astype(BF16)


def _ffn_mixer_kernel(xa_ref, xb_ref, wg_hbm, wu_hbm, wd_hbm, win_hbm,
                      n1_ref, n2_ref, wgz_ref, w2_ref, gb_ref, dw_ref, dwb_ref, lng_ref, lnb_ref, gng_ref,
                      cci_ref, si_ref,
                      x1_ref, m_ref, ccp_ref, sp_ref, ccs_ref, ss_ref,
                      wg_ref, wu_ref, wd_ref, win_ref, stage, sem, a_ref, h_ref, zbuf, hcarry,
                      qk_scr, v_scr, sg_scr, la_scr, s_scr, *, layer, n_prompt_tiles, tiles_per_seq):
    i = pl.program_id(0)
    j = jnp.maximum(i - 1, 0)
    ffn_is_prompt = i < n_prompt_tiles
    mix_is_prompt = j < n_prompt_tiles
    seq_tile = j % tiles_per_seq

    @pl.when(i == 0)
    def _():
        for ref in (zbuf, hcarry, qk_scr, v_scr, sg_scr, la_scr, s_scr):
            ref[...] = jnp.zeros(ref.shape, ref.dtype)
        _load_cast(wg_hbm.at[layer], wg_ref, stage, sem, D_MODEL, D_FF)
        _load_cast(wu_hbm.at[layer], wu_ref, stage, sem, D_MODEL, D_FF)
        _load_cast(wd_hbm.at[layer], wd_ref, stage, sem, D_FF, D_MODEL)
        _load_cast(win_hbm.at[layer], win_ref, stage, sem, D_MODEL, N_MAIN)

    @pl.when(mix_is_prompt & (seq_tile == 0))
    def _():
        hcarry[...] = jnp.zeros(hcarry.shape, F32)
        s_scr[...] = jnp.zeros(s_scr.shape, F32)

    per_group = N_CHUNK // len(FF_GROUPS)

    def conv_group(gi):
        def run():
            for c in range(gi * per_group, (gi + 1) * per_group):
                for sl in range(N_SLAB):
                    prev = hcarry[sl] if c == 0 else zbuf[sl, (c - 1) * SPAN + CHUNK:c * SPAN, :]
                    zbuf[sl, c * SPAN:c * SPAN + HIST, :] = jnp.where(
                        mix_is_prompt, prev, cci_ref[c, :, sl * LANES:(sl + 1) * LANES])
                _conv_chunk(c * SPAN + HIST, slice(c * CHUNK, (c + 1) * CHUNK), zbuf,
                            dw_ref, dwb_ref, lng_ref, lnb_ref, m_ref)
                for sl in range(N_SLAB):
                    ccs_ref[c, :, sl * LANES:(sl + 1) * LANES] = zbuf[sl, c * SPAN + CHUNK:(c + 1) * SPAN, :]
        return run

    def state_in(c, s_prev):
        return jnp.where(mix_is_prompt, s_scr[...] if c == 0 else s_prev, si_ref[c])

    def store_state(c, s):
        ss_ref[c] = s
        if c == N_CHUNK - 1:
            s_scr[...] = s

    def gla_work():
        return _gla_tile(state_in, store_state, qk_scr, v_scr, sg_scr, la_scr, gng_ref, m_ref)

    def x_tile():
        return jnp.where(ffn_is_prompt, xa_ref[...], xb_ref[...])

    h_ref[...] = _rms(x_tile(), n1_ref[...]).astype(BF16)
    for gi, (lo, hi) in enumerate(FF_GROUPS):
        @pl.when(i >= -gi)
        def _():
            gen = gla_work() if gi == 0 else None
            _gate_up(h_ref[...], wg_ref, wu_ref, a_ref, lo, hi, gen)
            if gen is not None:
                for _ in gen:
                    pass
            conv_group(gi)()
    x1 = x_tile() + 0.5 * _dot(a_ref[...], wd_ref[...])
    x1_ref[...] = x1

    h2 = _rms(x1, n2_ref[...]).astype(BF16)
    ua = _dot(h2, win_ref[:, 0:C_CONV])
    ub = _dot(h2, win_ref[:, C_CONV:2 * C_CONV])
    z = ua * _sigmoid(ub)
    hcarry[...] = zbuf[:, (N_CHUNK - 1) * SPAN + CHUNK:N_CHUNK * SPAN, :]
    for c in range(N_CHUNK):
        for sl in range(N_SLAB):
            zbuf[sl, c * SPAN + HIST:(c + 1) * SPAN, :] = z[c * CHUNK:(c + 1) * CHUNK, sl * LANES:(sl + 1) * LANES]
    o = 2 * C_CONV
    qk_scr[:, 0:QK_W] = _dot(h2, win_ref[:, o:o + QK_W]) * (DK ** -0.5)
    qk_scr[:, QK_W:2 * QK_W] = _dot(h2, win_ref[:, o + QK_W:o + 2 * QK_W])
    o += 2 * QK_W
    v_scr[...] = _dot(h2, win_ref[:, o:o + V_W]).astype(BF16)
    g = _dot(h2, win_ref[:, o + V_W:o + 2 * V_W])
    sg_scr[...] = g * _sigmoid(g)
    gz = _dot(h2, wgz_ref[...])
    pre = jnp.dot(gz, w2_ref[...], preferred_element_type=F32, precision=lax.Precision.HIGHEST) + gb_ref[...]
    la_scr[...] = (jnp.minimum(pre, 0.0) - jnp.log(1.0 + jnp.exp(-jnp.abs(pre)))) * (1.0 / GATE_TAU)

    @pl.when(mix_is_prompt & (seq_tile == tiles_per_seq - 1))
    def _():
        b = j // tiles_per_seq
        sp_ref[b] = s_scr[...]
        for sl in range(N_SLAB):
            ccp_ref[b, :, sl * LANES:(sl + 1) * LANES] = hcarry[sl]


def _ffn_mixer(xa, xb, xb_tile0, cc_in, s_in, w, layer, n_prompt_tiles, n_sample_tiles, tiles_per_seq, n_prompt_seq):
    n_tiles = n_prompt_tiles + n_sample_tiles
    sample_tile = lambda t: jnp.clip(t - n_prompt_tiles, 0, n_sample_tiles - 1)
    row = lambda f: pl.BlockSpec((TM, D_MODEL), lambda i: (f(i), 0))
    per_seq = lambda r, c: pl.BlockSpec((N_CHUNK, r, c), lambda i: (sample_tile(i - 1), 0, 0))
    once = pl.Buffered(1)
    per_seq_in = lambda r, c: pl.BlockSpec((None, N_CHUNK, r, c), lambda i: (layer, sample_tile(i - 1), 0, 0),
                                           pipeline_mode=once)
    xb_spec = pl.BlockSpec((TM, D_MODEL), lambda i: (xb_tile0 + sample_tile(i), 0), pipeline_mode=once)
    n_sample_seq = n_sample_tiles * N_CHUNK
    return pl.pallas_call(
        functools.partial(_ffn_mixer_kernel, layer=layer, n_prompt_tiles=n_prompt_tiles, tiles_per_seq=tiles_per_seq),
        grid=(n_tiles + 1,),
        in_specs=[row(lambda i: jnp.minimum(i, n_prompt_tiles - 1)), xb_spec]
        + [_HBM] * 4 + [_VMEM] * 10 + [per_seq_in(HIST, C_CONV), per_seq_in(QK_W, DV)],
        out_specs=[row(lambda i: jnp.minimum(i, n_tiles - 1)), row(lambda i: jnp.maximum(i - 1, 0)),
                   _VMEM, _VMEM, per_seq(HIST, C_CONV), per_seq(QK_W, DV)],
        out_shape=[jax.ShapeDtypeStruct((n_tiles * TM, D_MODEL), F32),
                   jax.ShapeDtypeStruct((n_tiles * TM, D_MODEL), BF16),
                   jax.ShapeDtypeStruct((n_prompt_seq, HIST, C_CONV), F32),
                   jax.ShapeDtypeStruct((n_prompt_seq, QK_W, DV), F32),
                   jax.ShapeDtypeStruct((n_sample_seq, HIST, C_CONV), F32),
                   jax.ShapeDtypeStruct((n_sample_seq, QK_W, DV), F32)],
        scratch_shapes=[pltpu.VMEM((D_MODEL, D_FF), BF16), pltpu.VMEM((D_MODEL, D_FF), BF16),
                        pltpu.VMEM((D_FF, D_MODEL), BF16), pltpu.VMEM((D_MODEL, N_MAIN), BF16),
                        pltpu.VMEM((2, W_ROWS, D_FF), F32), pltpu.SemaphoreType.DMA((2,)),
                        pltpu.VMEM((TM, D_FF), BF16), pltpu.VMEM((TM, D_MODEL), BF16),
                        pltpu.VMEM((N_SLAB, N_CHUNK * SPAN, LANES), F32), pltpu.VMEM((N_SLAB, HIST, LANES), F32),
                        pltpu.VMEM((TM, 2 * QK_W), F32), pltpu.VMEM((TM, V_W), BF16),
                        pltpu.VMEM((TM, V_W), F32), pltpu.VMEM((TM, QK_W), F32),
                        pltpu.VMEM((QK_W, DV), F32)],
        compiler_params=pltpu.CompilerParams(dimension_semantics=("arbitrary",), vmem_limit_bytes=VMEM_LIMIT),
        name="ffn_mixer",
    )(xa, xb, w["wg1"], w["wu1"], w["wd1"], w["win"], w["n1"][layer], w["n2"][layer], w["wgz"][layer],
      w["w2"][layer], w["gb"][layer], w["dw"][layer], w["dwb"][layer], w["lng"][layer], w["lnb"][layer],
      w["gng"][layer], cc_in, s_in)


def _outproj_ffn_kernel(x1_ref, m_ref, wo_hbm, wg_hbm, wu_hbm, wd_hbm, n3_ref, nf_ref, *rest,
                        layer, final, n_prompt_tiles):
    if final:
        yp_ref, ys_ref, wo_ref, wg_ref, wu_ref, wd_ref, stage, sem, a_ref = rest
    else:
        o_ref, wo_ref, wg_ref, wu_ref, wd_ref, stage, sem, a_ref = rest
    i = pl.program_id(0)

    @pl.when(i == 0)
    def _():
        _load_cast(wo_hbm.at[layer], wo_ref, stage, sem, D_MODEL, D_MODEL)
        _load_cast(wg_hbm.at[layer], wg_ref, stage, sem, D_MODEL, D_FF)
        _load_cast(wu_hbm.at[layer], wu_ref, stage, sem, D_MODEL, D_FF)
        _load_cast(wd_hbm.at[layer], wd_ref, stage, sem, D_FF, D_MODEL)

    x2 = x1_ref[...] + _dot(m_ref[...], wo_ref[...])
    h = _rms(x2, n3_ref[...]).astype(BF16)
    _gate_up(h, wg_ref, wu_ref, a_ref, 0, D_FF)
    x3 = x2 + 0.5 * _dot(a_ref[...], wd_ref[...])
    if final:
        y = _rms(x3, nf_ref[...])

        @pl.when(i < n_prompt_tiles)
        def _():
            yp_ref[...] = y

        @pl.when(i >= n_prompt_tiles)
        def _():
            ys_ref[...] = y
    else:
        o_ref[...] = x3


def _outproj_ffn(x1, mix, w, layer, final, n_prompt_tiles, n_sample_tiles):
    n_tiles = n_prompt_tiles + n_sample_tiles
    row = lambda f: pl.BlockSpec((TM, D_MODEL), lambda i: (f(i), 0))
    if final:
        out_specs = [row(lambda i: jnp.minimum(i, n_prompt_tiles - 1)),
                     row(lambda i: jnp.clip(i - n_prompt_tiles, 0, n_sample_tiles - 1))]
        out_shape = [jax.ShapeDtypeStruct((n_prompt_tiles * TM, D_MODEL), F32),
                     jax.ShapeDtypeStruct((n_sample_tiles * TM, D_MODEL), F32)]
    else:
        out_specs = row(lambda i: i)
        out_shape = jax.ShapeDtypeStruct((n_tiles * TM, D_MODEL), F32)
    return pl.pallas_call(
        functools.partial(_outproj_ffn_kernel, layer=layer, final=final, n_prompt_tiles=n_prompt_tiles),
        grid=(n_tiles,),
        in_specs=[row(lambda i: i), row(lambda i: i)] + [_HBM] * 4 + [_VMEM] * 2,
        out_specs=out_specs,
        out_shape=out_shape,
        scratch_shapes=[pltpu.VMEM((D_MODEL, D_MODEL), BF16), pltpu.VMEM((D_MODEL, D_FF), BF16),
                        pltpu.VMEM((D_MODEL, D_FF), BF16), pltpu.VMEM((D_FF, D_MODEL), BF16),
                        pltpu.VMEM((2, W_ROWS, D_FF), F32), pltpu.SemaphoreType.DMA((2,)),
                        pltpu.VMEM((TM, D_FF), BF16)],
        compiler_params=pltpu.CompilerParams(dimension_semantics=("arbitrary",), vmem_limit_bytes=VMEM_LIMIT),
        name="outproj_ffn",
    )(x1, mix, w["wo"], w["wg2"], w["wu2"], w["wd2"], w["n3"][layer], w["nf"])


def kernel(x_prompt, x_sample, cache_conv, state_gla, ffn1_norm, ffn1_w_gate, ffn1_w_up, ffn1_w_down, mix_norm, w_in, conv_dw_w, conv_dw_b, conv_ln_g, conv_ln_b, gla_gate_w2, gla_gate_b, gla_norm_g, w_out, ffn2_norm, ffn2_w_gate, ffn2_w_up, ffn2_w_down, final_norm):
    bp, lp, _ = x_prompt.shape
    bs, ls, _ = x_sample.shape
    assert lp % TM == 0 and ls == CHUNK and (bs * ls) % TM == 0
    n_prompt_tiles, n_sample_tiles, tiles_per_seq = bp * lp // TM, bs * ls // TM, lp // TM

    layers = lambda f: [f(l) for l in range(DEPTH)]
    row = lambda a: layers(lambda l: a[l].reshape(1, a.shape[-1]))
    w = dict(
        wg1=ffn1_w_gate, wu1=ffn1_w_up, wd1=ffn1_w_down, win=w_in, wo=w_out, wg2=ffn2_w_gate, wu2=ffn2_w_up,
        wd2=ffn2_w_down, n1=row(ffn1_norm), n2=row(mix_norm), n3=row(ffn2_norm), nf=final_norm.reshape(1, D_MODEL),
        wgz=layers(lambda l: jnp.pad(w_in[l, :, N_MAIN:], ((0, 0), (0, LANES - GATE_RANK))).astype(BF16)),
        w2=layers(lambda l: jnp.pad(gla_gate_w2[l], ((0, LANES - GATE_RANK), (0, 0)))), gb=row(gla_gate_b),
        dw=layers(lambda l: jnp.pad(conv_dw_w[l], ((0, 1), (0, 0)))), dwb=row(conv_dw_b), lng=row(conv_ln_g),
        lnb=row(conv_ln_b), gng=row(gla_norm_g),
    )
    cc_in = jnp.pad(cache_conv, ((0, 0), (0, 0), (HIST - (CONV_W - 1), 0), (0, 0)))
    s_in = state_gla.reshape(DEPTH, bs, QK_W, DV)

    xa, xb, xb_tile0 = x_prompt.reshape(bp * lp, D_MODEL), x_sample.reshape(bs * ls, D_MODEL), 0
    cc_p, s_p, cc_s, s_s = [], [], [], []
    for l in range(DEPTH):
        x1, mix, ccp, sp, ccs, ss = _ffn_mixer(xa, xb, xb_tile0, cc_in, s_in, w, l, n_prompt_tiles,
                                               n_sample_tiles, tiles_per_seq, bp)
        cc_p.append(ccp[:, HIST - (CONV_W - 1):, :])
        cc_s.append(ccs[:, HIST - (CONV_W - 1):, :])
        s_p.append(sp.reshape(bp, HEADS, DK, DV))
        s_s.append(ss.reshape(bs, HEADS, DK, DV))
        out = _outproj_ffn(x1, mix, w, l, l == DEPTH - 1, n_prompt_tiles, n_sample_tiles)
        xa, xb, xb_tile0 = out, out, n_prompt_tiles
    y_p, y_s = out
    return (y_p.reshape(bp, lp, D_MODEL), y_s.reshape(bs, ls, D_MODEL), jnp.stack(cc_p), jnp.stack(s_p),
            jnp.stack(cc_s), jnp.stack(s_s))
```

```python
import functools

import jax
import jax.numpy as jnp
from jax import lax
from jax.experimental import pallas as pl
from jax.experimental.pallas import tpu as pltpu

D_MODEL = 1024
DEPTH = 2
C_CONV = 512
CONV_W = 31
HEADS = 4
DK = 64
DV = 128
QK_W = HEADS * DK
V_W = HEADS * DV
N_MAIN = 2 * C_CONV + 2 * QK_W + 2 * V_W
GATE_RANK = 16
GATE_TAU = 16.0
D_FF = 2816
EPS = 1e-6
CHUNK = 64
SUB = 16
NSUB = CHUNK // SUB

LANES = 128
N_SLAB = C_CONV // LANES
HIST = 32
SPAN = HIST + CHUNK
FF_CHUNK = 256
FF_GROUPS = ((0, 768), (768, 1536), (1536, 2304), (2304, D_FF))
TM = 512
N_CHUNK = TM // CHUNK
VMEM_LIMIT = 58 * 1024 * 1024

F32 = jnp.float32
BF16 = jnp.bfloat16

_VMEM = pl.BlockSpec(memory_space=pltpu.VMEM)
_HBM = pl.BlockSpec(memory_space=pl.ANY)


def _sigmoid(x):
    return 1.0 / (1.0 + jnp.exp(-x))


def _rms(x, g):
    return x * lax.rsqrt(jnp.mean(x * x, axis=-1, keepdims=True) + EPS) * g


def _dot(a, b):
    return jnp.dot(a, b, preferred_element_type=F32)


def _load_weights(layer, pairs, sem):
    copies = [pltpu.make_async_copy(src.at[layer], dst, sem.at[k]) for k, (src, dst) in enumerate(pairs)]
    for c in copies:
        c.start()
    for c in copies:
        c.wait()


def _gate_up(h, wg_ref, wu_ref, a_ref, lo, hi, between=None):
    for c in range(lo, hi, FF_CHUNK):
        if between is not None:
            next(between, None)
        sl = slice(c, c + FF_CHUNK)
        g = _dot(h, wg_ref[:, sl])
        u = _dot(h, wu_ref[:, sl])
        a_ref[:, sl] = (g * _sigmoid(g) * u).astype(BF16)


def _gla_cumsum(la):
    r = lax.broadcasted_iota(jnp.int32, (CHUNK, CHUNK), 0)
    c = lax.broadcasted_iota(jnp.int32, (CHUNK, CHUNK), 1)
    tril = (r >= c).astype(BF16)
    la_hi = la.astype(BF16)
    r1 = la - la_hi.astype(F32)
    la_mid = r1.astype(BF16)
    la_lo = (r1 - la_mid.astype(F32)).astype(BF16)
    return _dot(tril, la_hi) + _dot(tril, la_mid) + _dot(tril, la_lo)


def _gla_operands(q, k, b):
    lane_head = lax.broadcasted_iota(jnp.int32, (CHUNK, QK_W), 1) // DK
    row_head = lax.broadcasted_iota(jnp.int32, (CHUNK, QK_W), 0) // SUB
    head_mask = lane_head == row_head
    j_idx = lax.broadcasted_iota(jnp.int32, (CHUNK, QK_W), 0)
    b_last = b[CHUNK - 1:CHUNK, :]
    qe = q * jnp.exp(b)
    qxs, kxs, qe_blocks = [], [], []
    for i in range(NSUB):
        rows = slice(i * SUB, (i + 1) * SUB)
        b_ref = b[i * SUB - 1:i * SUB, :] if i > 0 else jnp.zeros((1, QK_W), F32)
        qt = q[rows] * jnp.exp(b[rows] - b_ref)
        qxs.append(jnp.where(head_mask, jnp.concatenate([qt] * HEADS, axis=0), 0.0).astype(BF16))
        kxs.append(jnp.where(j_idx < (i + 1) * SUB, k * jnp.exp(b_ref - b), 0.0).astype(BF16))
        qe_blocks.append(jnp.where(head_mask, jnp.concatenate([qe[rows]] * HEADS, axis=0), 0.0).astype(BF16))
    qe_all = jnp.concatenate(qe_blocks, axis=0)
    kd = (k * jnp.exp(b_last - b)).astype(BF16)
    decay = jnp.exp(jnp.broadcast_to(b_last, (DV, QK_W)).T)
    return qxs, kxs, qe_all, kd, decay


def _gla_scores(qxs, kxs):
    r = lax.broadcasted_iota(jnp.int32, (CHUNK, CHUNK), 0)
    c = lax.broadcasted_iota(jnp.int32, (CHUNK, CHUNK), 1)
    row_local = r % SUB
    p_blocks = []
    for i in range(NSUB):
        a = lax.dot_general(qxs[i], kxs[i], (((1,), (1,)), ((), ())), preferred_element_type=F32)
        p_blocks.append(jnp.where(c <= i * SUB + row_local, a, 0.0).astype(BF16))
    return jnp.concatenate(p_blocks, axis=0)


def _gla_update(kd, v):
    upd = lax.dot_general(kd, v, (((0,), (0,)), ((), ())), preferred_element_type=F32)
    return jnp.concatenate([upd[h * DK:(h + 1) * DK, h * DV:(h + 1) * DV] for h in range(HEADS)], axis=0)


def _gla_output(p_all, qe_all, v, s):
    intra = _dot(p_all, v)
    inter = _dot(qe_all, s.astype(BF16))
    o_rows = []
    for i in range(NSUB):
        blocks = []
        for h in range(HEADS):
            r0 = i * CHUNK + h * SUB
            blocks.append(intra[r0:r0 + SUB, h * DV:(h + 1) * DV] + inter[r0:r0 + SUB, :])
        o_rows.append(jnp.concatenate(blocks, axis=1))
    return jnp.concatenate(o_rows, axis=0)


def _gla_tile(state_in, store_state, qk_scr, v_scr, sg_scr, la_scr, gng_ref, m_ref):
    rows = [slice(c * CHUNK, (c + 1) * CHUNK) for c in range(N_CHUNK)]
    bs = [_gla_cumsum(la_scr[r, :]) for r in rows]
    yield
    ops = [_gla_operands(qk_scr[r, 0:QK_W], qk_scr[r, QK_W:2 * QK_W], b) for r, b in zip(rows, bs)]
    ps = [_gla_scores(o[0], o[1]) for o in ops]
    upds = [_gla_update(o[3], v_scr[r, :]) for r, o in zip(rows, ops)]
    yield
    s_in, s_prev = [], None
    for c in range(N_CHUNK):
        s_in.append(state_in(c, s_prev))
        s_prev = ops[c][4] * s_in[c] + upds[c]
        store_state(c, s_prev)
    outs = [_gla_output(ps[c], ops[c][2], v_scr[rows[c], :], s_in[c]) for c in range(N_CHUNK)]
    yield
    for c in range(N_CHUNK):
        for h in range(HEADS):
            hs = slice(h * DV, (h + 1) * DV)
            oh = outs[c][:, hs]
            on = oh * lax.rsqrt(jnp.mean(oh * oh, axis=-1, keepdims=True) + EPS) * gng_ref[:, hs]
            m_ref[rows[c], C_CONV + h * DV:C_CONV + (h + 1) * DV] = (on * sg_scr[rows[c], hs]).astype(BF16)
    yield


def _conv_chunk(zrow, rows, zbuf, dw_ref, dwb_ref, lng_ref, lnb_ref, m_ref):
    acc = []
    for j in range(N_SLAB):
        ls = slice(j * LANES, (j + 1) * LANES)
        a = jnp.broadcast_to(dwb_ref[:, ls], (CHUNK, LANES))
        for w in range(CONV_W):
            a = a + zbuf[j, pl.ds(zrow - (CONV_W - 1) + w, CHUNK, stride=1), :] * dw_ref[w:w + 1, ls]
        acc.append(a)
    mu = sum(jnp.sum(a, axis=-1, keepdims=True) for a in acc) * (1.0 / C_CONV)
    xc = [a - mu for a in acc]
    rstd = lax.rsqrt(sum(jnp.sum(x * x, axis=-1, keepdims=True) for x in xc) * (1.0 / C_CONV) + EPS)
    for j in range(N_SLAB):
        ls = slice(j * LANES, (j + 1) * LANES)
        yn = xc[j] * rstd * lng_ref[:, ls] + lnb_ref[:, ls]
        m_ref[rows, ls] = (yn * _sigmoid(yn)).astype(BF16)


def _ffn_mixer_kernel(xa_ref, xb_ref, wg_hbm, wu_hbm, wd_hbm, win_hbm,
                      n1_ref, n2_ref, wgz_ref, w2_ref, gb_ref, dw_ref, dwb_ref, lng_ref, lnb_ref, gng_ref,
                      cci_ref, si_ref,
                      x1_ref, m_ref, ccp_ref, sp_ref, ccs_ref, ss_ref,
                      wg_ref, wu_ref, wd_ref, win_ref, sem, a_ref, h_ref, zbuf, hcarry,
                      qk_scr, v_scr, sg_scr, la_scr, s_scr, *, layer, n_prompt_tiles, tiles_per_seq):
    i = pl.program_id(0)
    j = jnp.maximum(i - 1, 0)
    ffn_is_prompt = i < n_prompt_tiles
    mix_is_prompt = j < n_prompt_tiles
    seq_tile = j % tiles_per_seq

    @pl.when(i == 0)
    def _():
        for ref in (zbuf, hcarry, qk_scr, v_scr, sg_scr, la_scr, s_scr):
            ref[...] = jnp.zeros(ref.shape, ref.dtype)
        _load_weights(layer, ((wg_hbm, wg_ref), (wu_hbm, wu_ref), (wd_hbm, wd_ref), (win_hbm, win_ref)), sem)

    @pl.when(mix_is_prompt & (seq_tile == 0))
    def _():
        hcarry[...] = jnp.zeros(hcarry.shape, F32)
        s_scr[...] = jnp.zeros(s_scr.shape, F32)

    per_group = N_CHUNK // len(FF_GROUPS)

    def conv_group(gi):
        def run():
            for c in range(gi * per_group, (gi + 1) * per_group):
                for sl in range(N_SLAB):
                    prev = hcarry[sl] if c == 0 else zbuf[sl, (c - 1) * SPAN + CHUNK:c * SPAN, :]
                    zbuf[sl, c * SPAN:c * SPAN + HIST, :] = jnp.where(
                        mix_is_prompt, prev, cci_ref[c, :, sl * LANES:(sl + 1) * LANES])
                _conv_chunk(c * SPAN + HIST, slice(c * CHUNK, (c + 1) * CHUNK), zbuf,
                            dw_ref, dwb_ref, lng_ref, lnb_ref, m_ref)
                for sl in range(N_SLAB):
                    ccs_ref[c, :, sl * LANES:(sl + 1) * LANES] = zbuf[sl, c * SPAN + CHUNK:(c + 1) * SPAN, :]
        return run

    def state_in(c, s_prev):
        return jnp.where(mix_is_prompt, s_scr[...] if c == 0 else s_prev, si_ref[c])

    def store_state(c, s):
        ss_ref[c] = s
        if c == N_CHUNK - 1:
            s_scr[...] = s

    def gla_work():
        return _gla_tile(state_in, store_state, qk_scr, v_scr, sg_scr, la_scr, gng_ref, m_ref)

    def x_tile():
        return jnp.where(ffn_is_prompt, xa_ref[...], xb_ref[...])

    h_ref[...] = _rms(x_tile(), n1_ref[...]).astype(BF16)
    for gi, (lo, hi) in enumerate(FF_GROUPS):
        @pl.when(i >= -gi)
        def _():
            gen = gla_work() if gi == 0 else None
            _gate_up(h_ref[...], wg_ref, wu_ref, a_ref, lo, hi, gen)
            if gen is not None:
                for _ in gen:
                    pass
            conv_group(gi)()
    x1 = x_tile() + 0.5 * _dot(a_ref[...], wd_ref[...])
    x1_ref[...] = x1

    h2 = _rms(x1, n2_ref[...]).astype(BF16)
    ua = _dot(h2, win_ref[:, 0:C_CONV])
    ub = _dot(h2, win_ref[:, C_CONV:2 * C_CONV])
    z = ua * _sigmoid(ub)
    hcarry[...] = zbuf[:, (N_CHUNK - 1) * SPAN + CHUNK:N_CHUNK * SPAN, :]
    for c in range(N_CHUNK):
        for sl in range(N_SLAB):
            zbuf[sl, c * SPAN + HIST:(c + 1) * SPAN, :] = z[c * CHUNK:(c + 1) * CHUNK, sl * LANES:(sl + 1) * LANES]
    o = 2 * C_CONV
    qk_scr[:, 0:QK_W] = _dot(h2, win_ref[:, o:o + QK_W]) * (DK ** -0.5)
    qk_scr[:, QK_W:2 * QK_W] = _dot(h2, win_ref[:, o + QK_W:o + 2 * QK_W])
    o += 2 * QK_W
    v_scr[...] = _dot(h2, win_ref[:, o:o + V_W]).astype(BF16)
    g = _dot(h2, win_ref[:, o + V_W:o + 2 * V_W])
    sg_scr[...] = g * _sigmoid(g)
    gz = _dot(h2, wgz_ref[...])
    pre = jnp.dot(gz, w2_ref[...], preferred_element_type=F32, precision=lax.Precision.HIGHEST) + gb_ref[...]
    la_scr[...] = (jnp.minimum(pre, 0.0) - jnp.log(1.0 + jnp.exp(-jnp.abs(pre)))) * (1.0 / GATE_TAU)

    @pl.when(mix_is_prompt & (seq_tile == tiles_per_seq - 1))
    def _():
        b = j // tiles_per_seq
        sp_ref[b] = s_scr[...]
        for sl in range(N_SLAB):
            ccp_ref[b, :, sl * LANES:(sl + 1) * LANES] = hcarry[sl]


def _ffn_mixer(xa, xb, xb_tile0, cc_in, s_in, w, layer, n_prompt_tiles, n_sample_tiles, tiles_per_seq, n_prompt_seq):
    n_tiles = n_prompt_tiles + n_sample_tiles
    sample_tile = lambda t: jnp.clip(t - n_prompt_tiles, 0, n_sample_tiles - 1)
    row = lambda f: pl.BlockSpec((TM, D_MODEL), lambda i: (f(i), 0))
    per_seq = lambda r, c: pl.BlockSpec((N_CHUNK, r, c), lambda i: (sample_tile(i - 1), 0, 0))
    once = pl.Buffered(1)
    per_seq_in = lambda r, c: pl.BlockSpec((None, N_CHUNK, r, c), lambda i: (layer, sample_tile(i - 1), 0, 0),
                                           pipeline_mode=once)
    xb_spec = pl.BlockSpec((TM, D_MODEL), lambda i: (xb_tile0 + sample_tile(i), 0), pipeline_mode=once)
    n_sample_seq = n_sample_tiles * N_CHUNK
    return pl.pallas_call(
        functools.partial(_ffn_mixer_kernel, layer=layer, n_prompt_tiles=n_prompt_tiles, tiles_per_seq=tiles_per_seq),
        grid=(n_tiles + 1,),
        in_specs=[row(lambda i: jnp.minimum(i, n_prompt_tiles - 1)), xb_spec]
        + [_HBM] * 4 + [_VMEM] * 10 + [per_seq_in(HIST, C_CONV), per_seq_in(QK_W, DV)],
        out_specs=[row(lambda i: jnp.minimum(i, n_tiles - 1)), row(lambda i: jnp.maximum(i - 1, 0)),
                   _VMEM, _VMEM, per_seq(HIST, C_CONV), per_seq(QK_W, DV)],
        out_shape=[jax.ShapeDtypeStruct((n_tiles * TM, D_MODEL), F32),
                   jax.ShapeDtypeStruct((n_tiles * TM, D_MODEL), BF16),
                   jax.ShapeDtypeStruct((n_prompt_seq, HIST, C_CONV), F32),
                   jax.ShapeDtypeStruct((n_prompt_seq, QK_W, DV), F32),
                   jax.ShapeDtypeStruct((n_sample_seq, HIST, C_CONV), F32),
                   jax.ShapeDtypeStruct((n_sample_seq, QK_W, DV), F32)],
        scratch_shapes=[pltpu.VMEM((D_MODEL, D_FF), BF16), pltpu.VMEM((D_MODEL, D_FF), BF16),
                        pltpu.VMEM((D_FF, D_MODEL), BF16), pltpu.VMEM((D_MODEL, N_MAIN), BF16),
                        pltpu.SemaphoreType.DMA((4,)),
                        pltpu.VMEM((TM, D_FF), BF16), pltpu.VMEM((TM, D_MODEL), BF16),
                        pltpu.VMEM((N_SLAB, N_CHUNK * SPAN, LANES), F32), pltpu.VMEM((N_SLAB, HIST, LANES), F32),
                        pltpu.VMEM((TM, 2 * QK_W), F32), pltpu.VMEM((TM, V_W), BF16),
                        pltpu.VMEM((TM, V_W), F32), pltpu.VMEM((TM, QK_W), F32),
                        pltpu.VMEM((QK_W, DV), F32)],
        compiler_params=pltpu.CompilerParams(dimension_semantics=("arbitrary",), vmem_limit_bytes=VMEM_LIMIT),
        name="ffn_mixer",
    )(xa, xb, w["wg1"], w["wu1"], w["wd1"], w["win"], w["n1"][layer], w["n2"][layer], w["wgz"][layer],
      w["w2"][layer], w["gb"][layer], w["dw"][layer], w["dwb"][layer], w["lng"][layer], w["lnb"][layer],
      w["gng"][layer], cc_in, s_in)


def _outproj_ffn_kernel(x1_ref, m_ref, wo_hbm, wg_hbm, wu_hbm, wd_hbm, n3_ref, nf_ref, *rest,
                        layer, final, n_prompt_tiles):
    if final:
        yp_ref, ys_ref, wo_ref, wg_ref, wu_ref, wd_ref, sem, a_ref = rest
    else:
        o_ref, wo_ref, wg_ref, wu_ref, wd_ref, sem, a_ref = rest
    i = pl.program_id(0)

    @pl.when(i == 0)
    def _():
        _load_weights(layer, ((wo_hbm, wo_ref), (wg_hbm, wg_ref), (wu_hbm, wu_ref), (wd_hbm, wd_ref)), sem)

    x2 = x1_ref[...] + _dot(m_ref[...], wo_ref[...])
    h = _rms(x2, n3_ref[...]).astype(BF16)
    _gate_up(h, wg_ref, wu_ref, a_ref, 0, D_FF)
    x3 = x2 + 0.5 * _dot(a_ref[...], wd_ref[...])
    if final:
        y = _rms(x3, nf_ref[...])

        @pl.when(i < n_prompt_tiles)
        def _():
            yp_ref[...] = y

        @pl.when(i >= n_prompt_tiles)
        def _():
            ys_ref[...] = y
    else:
        o_ref[...] = x3


def _outproj_ffn(x1, mix, w, layer, final, n_prompt_tiles, n_sample_tiles):
    n_tiles = n_prompt_tiles + n_sample_tiles
    row = lambda f: pl.BlockSpec((TM, D_MODEL), lambda i: (f(i), 0))
    if final:
        out_specs = [row(lambda i: jnp.minimum(i, n_prompt_tiles - 1)),
                     row(lambda i: jnp.clip(i - n_prompt_tiles, 0, n_sample_tiles - 1))]
        out_shape = [jax.ShapeDtypeStruct((n_prompt_tiles * TM, D_MODEL), F32),
                     jax.ShapeDtypeStruct((n_sample_tiles * TM, D_MODEL), F32)]
    else:
        out_specs = row(lambda i: i)
        out_shape = jax.ShapeDtypeStruct((n_tiles * TM, D_MODEL), F32)
    return pl.pallas_call(
        functools.partial(_outproj_ffn_kernel, layer=layer, final=final, n_prompt_tiles=n_prompt_tiles),
        grid=(n_tiles,),
        in_specs=[row(lambda i: i), row(lambda i: i)] + [_HBM] * 4 + [_VMEM] * 2,
        out_specs=out_specs,
        out_shape=out_shape,
        scratch_shapes=[pltpu.VMEM((D_MODEL, D_MODEL), BF16), pltpu.VMEM((D_MODEL, D_FF), BF16),
                        pltpu.VMEM((D_MODEL, D_FF), BF16), pltpu.VMEM((D_FF, D_MODEL), BF16),
                        pltpu.SemaphoreType.DMA((4,)),
                        pltpu.VMEM((TM, D_FF), BF16)],
        compiler_params=pltpu.CompilerParams(dimension_semantics=("arbitrary",), vmem_limit_bytes=VMEM_LIMIT),
        name="outproj_ffn",
    )(x1, mix, w["wo"], w["wg2"], w["wu2"], w["wd2"], w["n3"][layer], w["nf"])


def kernel(x_prompt, x_sample, cache_conv, state_gla, ffn1_norm, ffn1_w_gate, ffn1_w_up, ffn1_w_down, mix_norm, w_in, conv_dw_w, conv_dw_b, conv_ln_g, conv_ln_b, gla_gate_w2, gla_gate_b, gla_norm_g, w_out, ffn2_norm, ffn2_w_gate, ffn2_w_up, ffn2_w_down, final_norm):
    bp, lp, _ = x_prompt.shape
    bs, ls, _ = x_sample.shape
    assert lp % TM == 0 and ls == CHUNK and (bs * ls) % TM == 0
    n_prompt_tiles, n_sample_tiles, tiles_per_seq = bp * lp // TM, bs * ls // TM, lp // TM

    layers = lambda f: [f(l) for l in range(DEPTH)]
    row = lambda a: layers(lambda l: a[l].reshape(1, a.shape[-1]))
    w = dict(
        wg1=ffn1_w_gate.astype(BF16), wu1=ffn1_w_up.astype(BF16), wd1=ffn1_w_down.astype(BF16),
        win=w_in[:, :, :N_MAIN].astype(BF16), wo=w_out.astype(BF16), wg2=ffn2_w_gate.astype(BF16),
        wu2=ffn2_w_up.astype(BF16), wd2=ffn2_w_down.astype(BF16), n1=row(ffn1_norm), n2=row(mix_norm), n3=row(ffn2_norm), nf=final_norm.reshape(1, D_MODEL),
        wgz=layers(lambda l: jnp.pad(w_in[l, :, N_MAIN:], ((0, 0), (0, LANES - GATE_RANK))).astype(BF16)),
        w2=layers(lambda l: jnp.pad(gla_gate_w2[l], ((0, LANES - GATE_RANK), (0, 0)))), gb=row(gla_gate_b),
        dw=layers(lambda l: jnp.pad(conv_dw_w[l], ((0, 1), (0, 0)))), dwb=row(conv_dw_b), lng=row(conv_ln_g),
        lnb=row(conv_ln_b), gng=row(gla_norm_g),
    )
    cc_in = jnp.pad(cache_conv, ((0, 0), (0, 0), (HIST - (CONV_W - 1), 0), (0, 0)))
    s_in = state_gla.reshape(DEPTH, bs, QK_W, DV)

    xa, xb, xb_tile0 = x_prompt.reshape(bp * lp, D_MODEL), x_sample.reshape(bs * ls, D_MODEL), 0
    cc_p, s_p, cc_s, s_s = [], [], [], []
    for l in range(DEPTH):
        x1, mix, ccp, sp, ccs, ss = _ffn_mixer(xa, xb, xb_tile0, cc_in, s_in, w, l, n_prompt_tiles,
                                               n_sample_tiles, tiles_per_seq, bp)
        cc_p.append(ccp[:, HIST - (CONV_W - 1):, :])
        cc_s.append(ccs[:, HIST - (CONV_W - 1):, :])
        s_p.append(sp.reshape(bp, HEADS, DK, DV))
        s_s.append(ss.reshape(bs, HEADS, DK, DV))
        out = _outproj_ffn(x1, mix, w, l, l == DEPTH - 1, n_prompt_tiles, n_sample_tiles)
        xa, xb, xb_tile0 = out, out, n_prompt_tiles
    y_p, y_s = out
    return (y_p.reshape(bp, lp, D_MODEL), y_s.reshape(bs, ls, D_MODEL), jnp.stack(cc_p), jnp.stack(s_p),
            jnp.stack(cc_s), jnp.stack(s_s))
```

```python
import functools

import jax
import jax.numpy as jnp
from jax import lax
from jax.experimental import pallas as pl
from jax.experimental.pallas import tpu as pltpu

D_MODEL = 1024
DEPTH = 2
C_CONV = 512
CONV_W = 31
HEADS = 4
DK = 64
DV = 128
QK_W = HEADS * DK
V_W = HEADS * DV
N_MAIN = 2 * C_CONV + 2 * QK_W + 2 * V_W
GATE_RANK = 16
GATE_TAU = 16.0
D_FF = 2816
EPS = 1e-6
CHUNK = 64
SUB = 16
NSUB = CHUNK // SUB
FAST_LOG_DECAY_MIN = -4.0

LANES = 128
N_SLAB = C_CONV // LANES
HIST = 32
SPAN = HIST + CHUNK
FF_CHUNK = 256
FF_GROUPS = ((0, 768), (768, 1536), (1536, 2304), (2304, D_FF))
TM = 512
N_CHUNK = TM // CHUNK
VMEM_LIMIT = 58 * 1024 * 1024

F32 = jnp.float32
BF16 = jnp.bfloat16

_VMEM = pl.BlockSpec(memory_space=pltpu.VMEM)
_HBM = pl.BlockSpec(memory_space=pl.ANY)


def _sigmoid(x):
    return 1.0 / (1.0 + jnp.exp(-x))


def _rms(x, g):
    return x * lax.rsqrt(jnp.mean(x * x, axis=-1, keepdims=True) + EPS) * g


def _dot(a, b):
    return jnp.dot(a, b, preferred_element_type=F32)


def _load_weights(pairs, sem):
    copies = [pltpu.make_async_copy(src, dst, sem.at[k]) for k, (src, dst) in enumerate(pairs)]
    for c in copies:
        c.start()
    for c in copies:
        c.wait()


def _gate_up(h, wg_ref, wu_ref, a_ref, lo, hi, between=None):
    for c in range(lo, hi, FF_CHUNK):
        if between is not None:
            next(between, None)
        sl = slice(c, c + FF_CHUNK)
        g = _dot(h, wg_ref[:, sl])
        u = _dot(h, wu_ref[:, sl])
        a_ref[:, sl] = (g * _sigmoid(g) * u).astype(BF16)


def _gla_cumsum(la):
    r = lax.broadcasted_iota(jnp.int32, (CHUNK, CHUNK), 0)
    c = lax.broadcasted_iota(jnp.int32, (CHUNK, CHUNK), 1)
    tril = (r >= c).astype(BF16)
    la_hi = la.astype(BF16)
    r1 = la - la_hi.astype(F32)
    la_mid = r1.astype(BF16)
    la_lo = (r1 - la_mid.astype(F32)).astype(BF16)
    return _dot(tril, la_hi) + _dot(tril, la_mid) + _dot(tril, la_lo)


def _gla_operands(q, k, b):
    lane_head = lax.broadcasted_iota(jnp.int32, (CHUNK, QK_W), 1) // DK
    row_head = lax.broadcasted_iota(jnp.int32, (CHUNK, QK_W), 0) // SUB
    head_mask = lane_head == row_head
    j_idx = lax.broadcasted_iota(jnp.int32, (CHUNK, QK_W), 0)
    b_last = b[CHUNK - 1:CHUNK, :]
    qe = q * jnp.exp(b)
    qxs, kxs, qe_blocks = [], [], []
    for i in range(NSUB):
        rows = slice(i * SUB, (i + 1) * SUB)
        b_ref = b[i * SUB - 1:i * SUB, :] if i > 0 else jnp.zeros((1, QK_W), F32)
        qt = q[rows] * jnp.exp(b[rows] - b_ref)
        qxs.append(jnp.where(head_mask, jnp.concatenate([qt] * HEADS, axis=0), 0.0).astype(BF16))
        kxs.append(jnp.where(j_idx < (i + 1) * SUB, k * jnp.exp(b_ref - b), 0.0).astype(BF16))
        qe_blocks.append(jnp.where(head_mask, jnp.concatenate([qe[rows]] * HEADS, axis=0), 0.0).astype(BF16))
    qe_all = jnp.concatenate(qe_blocks, axis=0)
    kd = (k * jnp.exp(b_last - b)).astype(BF16)
    decay = jnp.exp(jnp.broadcast_to(b_last, (DV, QK_W)).T)
    return qxs, kxs, qe_all, kd, decay


def _gla_scores(qxs, kxs):
    r = lax.broadcasted_iota(jnp.int32, (CHUNK, CHUNK), 0)
    c = lax.broadcasted_iota(jnp.int32, (CHUNK, CHUNK), 1)
    row_local = r % SUB
    p_blocks = []
    for i in range(NSUB):
        a = lax.dot_general(qxs[i], kxs[i], (((1,), (1,)), ((), ())), preferred_element_type=F32)
        p_blocks.append(jnp.where(c <= i * SUB + row_local, a, 0.0).astype(BF16))
    return jnp.concatenate(p_blocks, axis=0)


def _gla_update(kd, v):
    upd = lax.dot_general(kd, v, (((0,), (0,)), ((), ())), preferred_element_type=F32)
    return jnp.concatenate([upd[h * DK:(h + 1) * DK, h * DV:(h + 1) * DV] for h in range(HEADS)], axis=0)


def _gla_output(p_all, qe_all, v, s):
    intra = _dot(p_all, v)
    inter = _dot(qe_all, s.astype(BF16))
    o_rows = []
    for i in range(NSUB):
        blocks = []
        for h in range(HEADS):
            r0 = i * CHUNK + h * SUB
            blocks.append(intra[r0:r0 + SUB, h * DV:(h + 1) * DV] + inter[r0:r0 + SUB, :])
        o_rows.append(jnp.concatenate(blocks, axis=1))
    return jnp.concatenate(o_rows, axis=0)


def _gla_tile(state_in, store_state, qk_scr, v_scr, sg_scr, la_scr, gng_ref, m_ref):
    rows = [slice(c * CHUNK, (c + 1) * CHUNK) for c in range(N_CHUNK)]
    bs = [_gla_cumsum(la_scr[r, :]) for r in rows]
    yield
    ops = [_gla_operands(qk_scr[r, 0:QK_W], qk_scr[r, QK_W:2 * QK_W], b) for r, b in zip(rows, bs)]
    ps = [_gla_scores(o[0], o[1]) for o in ops]
    upds = [_gla_update(o[3], v_scr[r, :]) for r, o in zip(rows, ops)]
    yield
    s_in, s_prev = [], None
    for c in range(N_CHUNK):
        s_in.append(state_in(c, s_prev))
        s_prev = ops[c][4] * s_in[c] + upds[c]
        store_state(c, s_prev)
    outs = [_gla_output(ps[c], ops[c][2], v_scr[rows[c], :], s_in[c]) for c in range(N_CHUNK)]
    yield
    for c in range(N_CHUNK):
        for h in range(HEADS):
            hs = slice(h * DV, (h + 1) * DV)
            oh = outs[c][:, hs]
            on = oh * lax.rsqrt(jnp.mean(oh * oh, axis=-1, keepdims=True) + EPS) * gng_ref[:, hs]
            m_ref[rows[c], C_CONV + h * DV:C_CONV + (h + 1) * DV] = (on * sg_scr[rows[c], hs]).astype(BF16)
    yield


def _gla_scores_exact(q, k_row, b, b_scr):
    b_scr[...] = b
    lane_head = lax.broadcasted_iota(jnp.int32, (CHUNK, QK_W), 1) // DK
    row_head = lax.broadcasted_iota(jnp.int32, (CHUNK, QK_W), 0) // SUB
    head_mask = lane_head == row_head
    q_rows, b_rows = [], []
    for i in range(NSUB):
        rows = slice(i * SUB, (i + 1) * SUB)
        q_rows.append(jnp.where(head_mask, jnp.concatenate([q[rows]] * HEADS, axis=0), 0.0))
        b_rows.append(jnp.concatenate([b[rows]] * HEADS, axis=0))
    q_rows = jnp.concatenate(q_rows, axis=0)
    b_rows = jnp.concatenate(b_rows, axis=0)
    n_rows = NSUB * CHUNK
    col = lax.broadcasted_iota(jnp.int32, (n_rows, LANES), 1)
    r = lax.broadcasted_iota(jnp.int32, (n_rows, LANES), 0)
    token = r // CHUNK * SUB + r % SUB

    def body(j, acc):
        p = q_rows * k_row(j) * jnp.exp(jnp.minimum(b_rows - b_scr[pl.ds(j, 1), :], 0.0))
        return jnp.where(col == j, jnp.sum(p, axis=-1, keepdims=True), acc)

    acc = lax.fori_loop(0, CHUNK, body, jnp.zeros((n_rows, LANES), F32))
    return jnp.where(col <= token, acc, 0.0)[:, 0:CHUNK].astype(BF16)


def _gla_tile_exact(state_in, store_state, qk_scr, v_scr, sg_scr, la_scr, gng_ref, m_ref, b_scr):
    s_prev = None
    for c in range(N_CHUNK):
        rows = slice(c * CHUNK, (c + 1) * CHUNK)
        q = qk_scr[rows, 0:QK_W]
        b = _gla_cumsum(la_scr[rows, :])
        _, _, qe_all, kd, decay = _gla_operands(q, qk_scr[rows, QK_W:2 * QK_W], b)
        p_all = _gla_scores_exact(q, lambda j: qk_scr[pl.ds(c * CHUNK + j, 1), QK_W:2 * QK_W], b, b_scr)
        s_in = state_in(c, s_prev)
        s_prev = decay * s_in + _gla_update(kd, v_scr[rows, :])
        store_state(c, s_prev)
        o = _gla_output(p_all, qe_all, v_scr[rows, :], s_in)
        for h in range(HEADS):
            hs = slice(h * DV, (h + 1) * DV)
            oh = o[:, hs]
            on = oh * lax.rsqrt(jnp.mean(oh * oh, axis=-1, keepdims=True) + EPS) * gng_ref[:, hs]
            m_ref[rows, C_CONV + h * DV:C_CONV + (h + 1) * DV] = (on * sg_scr[rows, hs]).astype(BF16)


def _conv_chunk(zrow, rows, zbuf, dw_ref, dwb_ref, lng_ref, lnb_ref, m_ref):
    acc = []
    for j in range(N_SLAB):
        ls = slice(j * LANES, (j + 1) * LANES)
        a = jnp.broadcast_to(dwb_ref[:, ls], (CHUNK, LANES))
        for w in range(CONV_W):
            a = a + zbuf[j, pl.ds(zrow - (CONV_W - 1) + w, CHUNK, stride=1), :] * dw_ref[w:w + 1, ls]
        acc.append(a)
    mu = sum(jnp.sum(a, axis=-1, keepdims=True) for a in acc) * (1.0 / C_CONV)
    xc = [a - mu for a in acc]
    rstd = lax.rsqrt(sum(jnp.sum(x * x, axis=-1, keepdims=True) for x in xc) * (1.0 / C_CONV) + EPS)
    for j in range(N_SLAB):
        ls = slice(j * LANES, (j + 1) * LANES)
        yn = xc[j] * rstd * lng_ref[:, ls] + lnb_ref[:, ls]
        m_ref[rows, ls] = (yn * _sigmoid(yn)).astype(BF16)


def _ffn_mixer_kernel(xa_ref, xb_ref, wg_hbm, wu_hbm, wd_hbm, win_hbm,
                      n1_ref, n2_ref, wgz_ref, w2_ref, gb_ref, dw_ref, dwb_ref, lng_ref, lnb_ref, gng_ref,
                      cci_ref, si_ref,
                      x1_ref, m_ref, ccp_ref, sp_ref, ccs_ref, ss_ref,
                      wg_ref, wu_ref, wd_ref, win_ref, sem, a_ref, h_ref, zbuf, hcarry,
                      qk_scr, v_scr, sg_scr, la_scr, s_scr, s_next, b_scr, *, layer, n_prompt_tiles, tiles_per_seq):
    i = pl.program_id(0)
    j = jnp.maximum(i - 1, 0)
    ffn_is_prompt = i < n_prompt_tiles
    mix_is_prompt = j < n_prompt_tiles
    seq_tile = j % tiles_per_seq

    @pl.when(i == 0)
    def _():
        for ref in (zbuf, hcarry, qk_scr, v_scr, sg_scr, la_scr, s_scr, s_next):
            ref[...] = jnp.zeros(ref.shape, ref.dtype)
        _load_weights(((wg_hbm.at[layer], wg_ref), (wu_hbm.at[layer], wu_ref), (wd_hbm.at[layer], wd_ref),
                       (win_hbm.at[layer, :, pl.ds(0, N_MAIN)], win_ref)), sem)

    @pl.when(mix_is_prompt & (seq_tile == 0))
    def _():
        hcarry[...] = jnp.zeros(hcarry.shape, F32)
        s_scr[...] = jnp.zeros(s_scr.shape, F32)

    per_group = N_CHUNK // len(FF_GROUPS)

    def conv_group(gi):
        def run():
            for c in range(gi * per_group, (gi + 1) * per_group):
                for sl in range(N_SLAB):
                    prev = hcarry[sl] if c == 0 else zbuf[sl, (c - 1) * SPAN + CHUNK:c * SPAN, :]
                    zbuf[sl, c * SPAN:c * SPAN + HIST, :] = jnp.where(
                        mix_is_prompt, prev, cci_ref[c, :, sl * LANES:(sl + 1) * LANES])
                _conv_chunk(c * SPAN + HIST, slice(c * CHUNK, (c + 1) * CHUNK), zbuf,
                            dw_ref, dwb_ref, lng_ref, lnb_ref, m_ref)
                for sl in range(N_SLAB):
                    ccs_ref[c, :, sl * LANES:(sl + 1) * LANES] = zbuf[sl, c * SPAN + CHUNK:(c + 1) * SPAN, :]
        return run

    def state_in(c, s_prev):
        return jnp.where(mix_is_prompt, s_scr[...] if c == 0 else s_prev, si_ref[c])

    def store_state(c, s):
        ss_ref[c] = s
        if c == N_CHUNK - 1:
            s_next[...] = s

    def gla_work():
        return _gla_tile(state_in, store_state, qk_scr, v_scr, sg_scr, la_scr, gng_ref, m_ref)

    need_exact = jnp.min(la_scr[...]) < FAST_LOG_DECAY_MIN

    def x_tile():
        return jnp.where(ffn_is_prompt, xa_ref[...], xb_ref[...])

    h_ref[...] = _rms(x_tile(), n1_ref[...]).astype(BF16)
    for gi, (lo, hi) in enumerate(FF_GROUPS):
        @pl.when(i >= -gi)
        def _():
            gen = gla_work() if gi == 0 else None
            _gate_up(h_ref[...], wg_ref, wu_ref, a_ref, lo, hi, gen)
            if gen is not None:
                for _ in gen:
                    pass
            conv_group(gi)()

    @pl.when(need_exact)
    def _():
        _gla_tile_exact(state_in, store_state, qk_scr, v_scr, sg_scr, la_scr, gng_ref, m_ref, b_scr)

    s_scr[...] = s_next[...]
    x1 = x_tile() + 0.5 * _dot(a_ref[...], wd_ref[...])
    x1_ref[...] = x1

    h2 = _rms(x1, n2_ref[...]).astype(BF16)
    ua = _dot(h2, win_ref[:, 0:C_CONV])
    ub = _dot(h2, win_ref[:, C_CONV:2 * C_CONV])
    z = ua * _sigmoid(ub)
    hcarry[...] = zbuf[:, (N_CHUNK - 1) * SPAN + CHUNK:N_CHUNK * SPAN, :]
    for c in range(N_CHUNK):
        for sl in range(N_SLAB):
            zbuf[sl, c * SPAN + HIST:(c + 1) * SPAN, :] = z[c * CHUNK:(c + 1) * CHUNK, sl * LANES:(sl + 1) * LANES]
    o = 2 * C_CONV
    qk_scr[:, 0:QK_W] = _dot(h2, win_ref[:, o:o + QK_W]) * (DK ** -0.5)
    qk_scr[:, QK_W:2 * QK_W] = _dot(h2, win_ref[:, o + QK_W:o + 2 * QK_W])
    o += 2 * QK_W
    v_scr[...] = _dot(h2, win_ref[:, o:o + V_W]).astype(BF16)
    g = _dot(h2, win_ref[:, o + V_W:o + 2 * V_W])
    sg_scr[...] = g * _sigmoid(g)
    gz = _dot(h2, wgz_ref[...])
    pre = jnp.dot(gz, w2_ref[...], preferred_element_type=F32, precision=lax.Precision.HIGHEST) + gb_ref[...]
    la_scr[...] = (jnp.minimum(pre, 0.0) - jnp.log(1.0 + jnp.exp(-jnp.abs(pre)))) * (1.0 / GATE_TAU)

    @pl.when(mix_is_prompt & (seq_tile == tiles_per_seq - 1))
    def _():
        b = j // tiles_per_seq
        sp_ref[b] = s_scr[...]
        for sl in range(N_SLAB):
            ccp_ref[b, :, sl * LANES:(sl + 1) * LANES] = hcarry[sl]


def _ffn_mixer(xa, xb, xb_tile0, cc_in, s_in, w, layer, n_prompt_tiles, n_sample_tiles, tiles_per_seq, n_prompt_seq):
    n_tiles = n_prompt_tiles + n_sample_tiles
    sample_tile = lambda t: jnp.clip(t - n_prompt_tiles, 0, n_sample_tiles - 1)
    row = lambda f: pl.BlockSpec((TM, D_MODEL), lambda i: (f(i), 0))
    per_seq = lambda r, c: pl.BlockSpec((N_CHUNK, r, c), lambda i: (sample_tile(i - 1), 0, 0))
    once = pl.Buffered(1)
    per_seq_in = lambda r, c: pl.BlockSpec((None, N_CHUNK, r, c), lambda i: (layer, sample_tile(i - 1), 0, 0),
                                           pipeline_mode=once)
    xb_spec = pl.BlockSpec((TM, D_MODEL), lambda i: (xb_tile0 + sample_tile(i), 0), pipeline_mode=once)
    n_sample_seq = n_sample_tiles * N_CHUNK
    return pl.pallas_call(
        functools.partial(_ffn_mixer_kernel, layer=layer, n_prompt_tiles=n_prompt_tiles, tiles_per_seq=tiles_per_seq),
        grid=(n_tiles + 1,),
        in_specs=[row(lambda i: jnp.minimum(i, n_prompt_tiles - 1)), xb_spec]
        + [_HBM] * 4 + [_VMEM] * 10 + [per_seq_in(HIST, C_CONV), per_seq_in(QK_W, DV)],
        out_specs=[row(lambda i: jnp.minimum(i, n_tiles - 1)), row(lambda i: jnp.maximum(i - 1, 0)),
                   _VMEM, _VMEM, per_seq(HIST, C_CONV), per_seq(QK_W, DV)],
        out_shape=[jax.ShapeDtypeStruct((n_tiles * TM, D_MODEL), F32),
                   jax.ShapeDtypeStruct((n_tiles * TM, D_MODEL), BF16),
                   jax.ShapeDtypeStruct((n_prompt_seq, HIST, C_CONV), F32),
                   jax.ShapeDtypeStruct((n_prompt_seq, QK_W, DV), F32),
                   jax.ShapeDtypeStruct((n_sample_seq, HIST, C_CONV), F32),
                   jax.ShapeDtypeStruct((n_sample_seq, QK_W, DV), F32)],
        scratch_shapes=[pltpu.VMEM((D_MODEL, D_FF), BF16), pltpu.VMEM((D_MODEL, D_FF), BF16),
                        pltpu.VMEM((D_FF, D_MODEL), BF16), pltpu.VMEM((D_MODEL, N_MAIN), BF16),
                        pltpu.SemaphoreType.DMA((4,)),
                        pltpu.VMEM((TM, D_FF), BF16), pltpu.VMEM((TM, D_MODEL), BF16),
                        pltpu.VMEM((N_SLAB, N_CHUNK * SPAN, LANES), F32), pltpu.VMEM((N_SLAB, HIST, LANES), F32),
                        pltpu.VMEM((TM, 2 * QK_W), F32), pltpu.VMEM((TM, V_W), BF16),
                        pltpu.VMEM((TM, V_W), F32), pltpu.VMEM((TM, QK_W), F32),
                        pltpu.VMEM((QK_W, DV), F32), pltpu.VMEM((QK_W, DV), F32), pltpu.VMEM((CHUNK, QK_W), F32)],
        compiler_params=pltpu.CompilerParams(dimension_semantics=("arbitrary",), vmem_limit_bytes=VMEM_LIMIT),
        name="ffn_mixer",
    )(xa, xb, w["wg1"], w["wu1"], w["wd1"], w["win"], w["n1"][layer], w["n2"][layer], w["wgz"][layer],
      w["w2"][layer], w["gb"][layer], w["dw"][layer], w["dwb"][layer], w["lng"][layer], w["lnb"][layer],
      w["gng"][layer], cc_in, s_in)


def _outproj_ffn_kernel(x1_ref, m_ref, wo_hbm, wg_hbm, wu_hbm, wd_hbm, n3_ref, nf_ref, *rest,
                        layer, final, n_prompt_tiles):
    if final:
        yp_ref, ys_ref, wo_ref, wg_ref, wu_ref, wd_ref, sem, a_ref = rest
    else:
        o_ref, wo_ref, wg_ref, wu_ref, wd_ref, sem, a_ref = rest
    i = pl.program_id(0)

    @pl.when(i == 0)
    def _():
        _load_weights(((wo_hbm.at[layer], wo_ref), (wg_hbm.at[layer], wg_ref), (wu_hbm.at[layer], wu_ref),
                       (wd_hbm.at[layer], wd_ref)), sem)

    x2 = x1_ref[...] + _dot(m_ref[...], wo_ref[...])
    h = _rms(x2, n3_ref[...]).astype(BF16)
    _gate_up(h, wg_ref, wu_ref, a_ref, 0, D_FF)
    x3 = x2 + 0.5 * _dot(a_ref[...], wd_ref[...])
    if final:
        y = _rms(x3, nf_ref[...])

        @pl.when(i < n_prompt_tiles)
        def _():
            yp_ref[...] = y

        @pl.when(i >= n_prompt_tiles)
        def _():
            ys_ref[...] = y
    else:
        o_ref[...] = x3


def _outproj_ffn(x1, mix, w, layer, final, n_prompt_tiles, n_sample_tiles):
    n_tiles = n_prompt_tiles + n_sample_tiles
    row = lambda f: pl.BlockSpec((TM, D_MODEL), lambda i: (f(i), 0))
    if final:
        out_specs = [row(lambda i: jnp.minimum(i, n_prompt_tiles - 1)),
                     row(lambda i: jnp.clip(i - n_prompt_tiles, 0, n_sample_tiles - 1))]
        out_shape = [jax.ShapeDtypeStruct((n_prompt_tiles * TM, D_MODEL), F32),
                     jax.ShapeDtypeStruct((n_sample_tiles * TM, D_MODEL), F32)]
    else:
        out_specs = row(lambda i: i)
        out_shape = jax.ShapeDtypeStruct((n_tiles * TM, D_MODEL), F32)
    return pl.pallas_call(
        functools.partial(_outproj_ffn_kernel, layer=layer, final=final, n_prompt_tiles=n_prompt_tiles),
        grid=(n_tiles,),
        in_specs=[row(lambda i: i), row(lambda i: i)] + [_HBM] * 4 + [_VMEM] * 2,
        out_specs=out_specs,
        out_shape=out_shape,
        scratch_shapes=[pltpu.VMEM((D_MODEL, D_MODEL), BF16), pltpu.VMEM((D_MODEL, D_FF), BF16),
                        pltpu.VMEM((D_MODEL, D_FF), BF16), pltpu.VMEM((D_FF, D_MODEL), BF16),
                        pltpu.SemaphoreType.DMA((4,)),
                        pltpu.VMEM((TM, D_FF), BF16)],
        compiler_params=pltpu.CompilerParams(dimension_semantics=("arbitrary",), vmem_limit_bytes=VMEM_LIMIT),
        name="outproj_ffn",
    )(x1, mix, w["wo"], w["wg2"], w["wu2"], w["wd2"], w["n3"][layer], w["nf"])


def kernel(x_prompt, x_sample, cache_conv, state_gla, ffn1_norm, ffn1_w_gate, ffn1_w_up, ffn1_w_down, mix_norm, w_in, conv_dw_w, conv_dw_b, conv_ln_g, conv_ln_b, gla_gate_w2, gla_gate_b, gla_norm_g, w_out, ffn2_norm, ffn2_w_gate, ffn2_w_up, ffn2_w_down, final_norm):
    bp, lp, _ = x_prompt.shape
    bs, ls, _ = x_sample.shape
    assert lp % TM == 0 and ls == CHUNK and (bs * ls) % TM == 0
    n_prompt_tiles, n_sample_tiles, tiles_per_seq = bp * lp // TM, bs * ls // TM, lp // TM

    layers = lambda f: [f(l) for l in range(DEPTH)]
    row = lambda a: layers(lambda l: a[l].reshape(1, a.shape[-1]))
    w = dict(
        wg1=ffn1_w_gate.astype(BF16), wu1=ffn1_w_up.astype(BF16), wd1=ffn1_w_down.astype(BF16),
        win=w_in.astype(BF16), wo=w_out.astype(BF16), wg2=ffn2_w_gate.astype(BF16),
        wu2=ffn2_w_up.astype(BF16), wd2=ffn2_w_down.astype(BF16), n1=row(ffn1_norm), n2=row(mix_norm), n3=row(ffn2_norm), nf=final_norm.reshape(1, D_MODEL),
        wgz=layers(lambda l: jnp.pad(w_in[l, :, N_MAIN:], ((0, 0), (0, LANES - GATE_RANK))).astype(BF16)),
        w2=layers(lambda l: jnp.pad(gla_gate_w2[l], ((0, LANES - GATE_RANK), (0, 0)))), gb=row(gla_gate_b),
        dw=layers(lambda l: jnp.pad(conv_dw_w[l], ((0, 1), (0, 0)))), dwb=row(conv_dw_b), lng=row(conv_ln_g),
        lnb=row(conv_ln_b), gng=row(gla_norm_g),
    )
    cc_in = jnp.pad(cache_conv, ((0, 0), (0, 0), (HIST - (CONV_W - 1), 0), (0, 0)))
    s_in = state_gla.reshape(DEPTH, bs, QK_W, DV)

    xa, xb, xb_tile0 = x_prompt.reshape(bp * lp, D_MODEL), x_sample.reshape(bs * ls, D_MODEL), 0
    cc_p, s_p, cc_s, s_s = [], [], [], []
    for l in range(DEPTH):
        x1, mix, ccp, sp, ccs, ss = _ffn_mixer(xa, xb, xb_tile0, cc_in, s_in, w, l, n_prompt_tiles,
                                               n_sample_tiles, tiles_per_seq, bp)
        cc_p.append(ccp[:, HIST - (CONV_W - 1):, :])
        cc_s.append(ccs[:, HIST - (CONV_W - 1):, :])
        s_p.append(sp.reshape(bp, HEADS, DK, DV))
        s_s.append(ss.reshape(bs, HEADS, DK, DV))
        out = _outproj_ffn(x1, mix, w, l, l == DEPTH - 1, n_prompt_tiles, n_sample_tiles)
        xa, xb, xb_tile0 = out, out, n_prompt_tiles
    y_p, y_s = out
    return (y_p.reshape(bp, lp, D_MODEL), y_s.reshape(bs, ls, D_MODEL), jnp.stack(cc_p), jnp.stack(s_p),
            jnp.stack(cc_s), jnp.stack(s_s))
```

```python
import functools

import jax
import jax.numpy as jnp
from jax import lax
from jax.experimental import pallas as pl
from jax.experimental.pallas import tpu as pltpu

D_MODEL = 1024
DEPTH = 2
C_CONV = 512
CONV_W = 31
HEADS = 4
DK = 64
DV = 128
QK_W = HEADS * DK
V_W = HEADS * DV
N_MAIN = 2 * C_CONV + 2 * QK_W + 2 * V_W
GATE_RANK = 16
GATE_TAU = 16.0
D_FF = 2816
EPS = 1e-6
CHUNK = 64
SUB = 16
NSUB = CHUNK // SUB
FAST_LOG_DECAY_MIN = -4.0

LANES = 128
N_SLAB = C_CONV // LANES
HIST = 32
SPAN = HIST + CHUNK
FF_CHUNK = 256
TM = 512
N_CHUNK = TM // CHUNK
VMEM_LIMIT = 58 * 1024 * 1024

F32 = jnp.float32
BF16 = jnp.bfloat16

_VMEM = pl.BlockSpec(memory_space=pltpu.VMEM)
_HBM = pl.BlockSpec(memory_space=pl.ANY)


def _sigmoid(x):
    return 1.0 / (1.0 + jnp.exp(-x))


def _rms(x, g):
    return x * lax.rsqrt(jnp.mean(x * x, axis=-1, keepdims=True) + EPS) * g


def _dot(a, b):
    return jnp.dot(a, b, preferred_element_type=F32)


def _load_weights(pairs, sem):
    copies = [pltpu.make_async_copy(src, dst, sem.at[k]) for k, (src, dst) in enumerate(pairs)]
    for c in copies:
        c.start()
    for c in copies:
        c.wait()


def _gate_up(h, wg_ref, wu_ref, a_ref, lo, hi, between=None):
    for c in range(lo, hi, FF_CHUNK):
        if between is not None:
            next(between, None)
        sl = slice(c, c + FF_CHUNK)
        g = _dot(h, wg_ref[:, sl])
        u = _dot(h, wu_ref[:, sl])
        a_ref[:, sl] = (g * _sigmoid(g) * u).astype(BF16)


def _gla_cumsum(la):
    r = lax.broadcasted_iota(jnp.int32, (CHUNK, CHUNK), 0)
    c = lax.broadcasted_iota(jnp.int32, (CHUNK, CHUNK), 1)
    tril = (r >= c).astype(BF16)
    la_hi = la.astype(BF16)
    r1 = la - la_hi.astype(F32)
    la_mid = r1.astype(BF16)
    la_lo = (r1 - la_mid.astype(F32)).astype(BF16)
    return _dot(tril, la_hi) + _dot(tril, la_mid) + _dot(tril, la_lo)


def _gla_operands(q, k, b):
    lane_head = lax.broadcasted_iota(jnp.int32, (CHUNK, QK_W), 1) // DK
    row_head = lax.broadcasted_iota(jnp.int32, (CHUNK, QK_W), 0) // SUB
    head_mask = lane_head == row_head
    j_idx = lax.broadcasted_iota(jnp.int32, (CHUNK, QK_W), 0)
    b_last = b[CHUNK - 1:CHUNK, :]
    qe = q * jnp.exp(b)
    qxs, kxs, qe_blocks = [], [], []
    for i in range(NSUB):
        rows = slice(i * SUB, (i + 1) * SUB)
        b_ref = b[i * SUB - 1:i * SUB, :] if i > 0 else jnp.zeros((1, QK_W), F32)
        qt = q[rows] * jnp.exp(b[rows] - b_ref)
        qxs.append(jnp.where(head_mask, jnp.concatenate([qt] * HEADS, axis=0), 0.0).astype(BF16))
        kxs.append(jnp.where(j_idx < (i + 1) * SUB, k * jnp.exp(b_ref - b), 0.0).astype(BF16))
        qe_blocks.append(jnp.where(head_mask, jnp.concatenate([qe[rows]] * HEADS, axis=0), 0.0).astype(BF16))
    qe_all = jnp.concatenate(qe_blocks, axis=0)
    kd = (k * jnp.exp(b_last - b)).astype(BF16)
    decay = jnp.exp(jnp.broadcast_to(b_last, (DV, QK_W)).T)
    return qxs, kxs, qe_all, kd, decay


def _gla_scores(qxs, kxs):
    r = lax.broadcasted_iota(jnp.int32, (CHUNK, CHUNK), 0)
    c = lax.broadcasted_iota(jnp.int32, (CHUNK, CHUNK), 1)
    row_local = r % SUB
    p_blocks = []
    for i in range(NSUB):
        a = lax.dot_general(qxs[i], kxs[i], (((1,), (1,)), ((), ())), preferred_element_type=F32)
        p_blocks.append(jnp.where(c <= i * SUB + row_local, a, 0.0).astype(BF16))
    return jnp.concatenate(p_blocks, axis=0)


def _gla_update(kd, v):
    upd = lax.dot_general(kd, v, (((0,), (0,)), ((), ())), preferred_element_type=F32)
    return jnp.concatenate([upd[h * DK:(h + 1) * DK, h * DV:(h + 1) * DV] for h in range(HEADS)], axis=0)


def _gla_output(p_all, qe_all, v, s):
    intra = _dot(p_all, v)
    inter = _dot(qe_all, s.astype(BF16))
    o_rows = []
    for i in range(NSUB):
        blocks = []
        for h in range(HEADS):
            r0 = i * CHUNK + h * SUB
            blocks.append(intra[r0:r0 + SUB, h * DV:(h + 1) * DV] + inter[r0:r0 + SUB, :])
        o_rows.append(jnp.concatenate(blocks, axis=1))
    return jnp.concatenate(o_rows, axis=0)


def _gla_tile(state_in, store_state, qk_scr, v_scr, sg_scr, la_scr, gng_ref, m_ref):
    rows = [slice(c * CHUNK, (c + 1) * CHUNK) for c in range(N_CHUNK)]
    bs = [_gla_cumsum(la_scr[r, :]) for r in rows]
    yield
    ops = [_gla_operands(qk_scr[r, 0:QK_W], qk_scr[r, QK_W:2 * QK_W], b) for r, b in zip(rows, bs)]
    ps = [_gla_scores(o[0], o[1]) for o in ops]
    upds = [_gla_update(o[3], v_scr[r, :]) for r, o in zip(rows, ops)]
    yield
    s_in, s_prev = [], None
    for c in range(N_CHUNK):
        s_in.append(state_in(c, s_prev))
        s_prev = ops[c][4] * s_in[c] + upds[c]
        store_state(c, s_prev)
    outs = [_gla_output(ps[c], ops[c][2], v_scr[rows[c], :], s_in[c]) for c in range(N_CHUNK)]
    yield
    for c in range(N_CHUNK):
        for h in range(HEADS):
            hs = slice(h * DV, (h + 1) * DV)
            oh = outs[c][:, hs]
            on = oh * lax.rsqrt(jnp.mean(oh * oh, axis=-1, keepdims=True) + EPS) * gng_ref[:, hs]
            m_ref[rows[c], C_CONV + h * DV:C_CONV + (h + 1) * DV] = (on * sg_scr[rows[c], hs]).astype(BF16)
    yield


def _gla_scores_exact(q, k_row, b, b_scr):
    b_scr[...] = b
    lane_head = lax.broadcasted_iota(jnp.int32, (CHUNK, QK_W), 1) // DK
    row_head = lax.broadcasted_iota(jnp.int32, (CHUNK, QK_W), 0) // SUB
    head_mask = lane_head == row_head
    q_rows, b_rows = [], []
    for i in range(NSUB):
        rows = slice(i * SUB, (i + 1) * SUB)
        q_rows.append(jnp.where(head_mask, jnp.concatenate([q[rows]] * HEADS, axis=0), 0.0))
        b_rows.append(jnp.concatenate([b[rows]] * HEADS, axis=0))
    q_rows = jnp.concatenate(q_rows, axis=0)
    b_rows = jnp.concatenate(b_rows, axis=0)
    n_rows = NSUB * CHUNK
    col = lax.broadcasted_iota(jnp.int32, (n_rows, LANES), 1)
    r = lax.broadcasted_iota(jnp.int32, (n_rows, LANES), 0)
    token = r // CHUNK * SUB + r % SUB

    def body(j, acc):
        p = q_rows * k_row(j) * jnp.exp(jnp.minimum(b_rows - b_scr[pl.ds(j, 1), :], 0.0))
        return jnp.where(col == j, jnp.sum(p, axis=-1, keepdims=True), acc)

    acc = lax.fori_loop(0, CHUNK, body, jnp.zeros((n_rows, LANES), F32))
    return jnp.where(col <= token, acc, 0.0)[:, 0:CHUNK].astype(BF16)


def _gla_tile_exact(state_in, store_state, qk_scr, v_scr, sg_scr, la_scr, gng_ref, m_ref, b_scr):
    s_prev = None
    for c in range(N_CHUNK):
        rows = slice(c * CHUNK, (c + 1) * CHUNK)
        q = qk_scr[rows, 0:QK_W]
        b = _gla_cumsum(la_scr[rows, :])
        _, _, qe_all, kd, decay = _gla_operands(q, qk_scr[rows, QK_W:2 * QK_W], b)
        p_all = _gla_scores_exact(q, lambda j: qk_scr[pl.ds(c * CHUNK + j, 1), QK_W:2 * QK_W], b, b_scr)
        s_in = state_in(c, s_prev)
        s_prev = decay * s_in + _gla_update(kd, v_scr[rows, :])
        store_state(c, s_prev)
        o = _gla_output(p_all, qe_all, v_scr[rows, :], s_in)
        for h in range(HEADS):
            hs = slice(h * DV, (h + 1) * DV)
            oh = o[:, hs]
            on = oh * lax.rsqrt(jnp.mean(oh * oh, axis=-1, keepdims=True) + EPS) * gng_ref[:, hs]
            m_ref[rows, C_CONV + h * DV:C_CONV + (h + 1) * DV] = (on * sg_scr[rows, hs]).astype(BF16)


def _conv_chunk(zrow, rows, zbuf, dw_ref, dwb_ref, lng_ref, lnb_ref, m_ref):
    acc = []
    for j in range(N_SLAB):
        ls = slice(j * LANES, (j + 1) * LANES)
        a = jnp.broadcast_to(dwb_ref[:, ls], (CHUNK, LANES))
        for w in range(CONV_W):
            a = a + zbuf[j, pl.ds(zrow - (CONV_W - 1) + w, CHUNK, stride=1), :] * dw_ref[w:w + 1, ls]
        acc.append(a)
    mu = sum(jnp.sum(a, axis=-1, keepdims=True) for a in acc) * (1.0 / C_CONV)
    xc = [a - mu for a in acc]
    rstd = lax.rsqrt(sum(jnp.sum(x * x, axis=-1, keepdims=True) for x in xc) * (1.0 / C_CONV) + EPS)
    for j in range(N_SLAB):
        ls = slice(j * LANES, (j + 1) * LANES)
        yn = xc[j] * rstd * lng_ref[:, ls] + lnb_ref[:, ls]
        m_ref[rows, ls] = (yn * _sigmoid(yn)).astype(BF16)


def _ffn_mixer_kernel(xa_ref, xb_ref, wg_hbm, wu_hbm, wd_hbm, win_hbm,
                      n1_ref, n2_ref, wgz_ref, w2_ref, gb_ref, dw_ref, dwb_ref, lng_ref, lnb_ref, gng_ref,
                      cci_ref, si_ref,
                      x1_ref, m_ref, ccp_ref, sp_ref, ccs_ref, ss_ref,
                      wg_ref, wu_ref, wd_ref, win_ref, sem, a_ref, h_ref, zbuf, hcarry,
                      qk_scr, v_scr, sg_scr, la_scr, s_scr, s_next, b_scr, wgate_ref,
                      *, layer, n_prompt_tiles, n_tiles, tiles_per_seq):
    i = pl.program_id(0)
    j = jnp.maximum(i - 1, 0)
    ffn_is_prompt = i < n_prompt_tiles
    mix_is_prompt = j < n_prompt_tiles
    seq_tile = j % tiles_per_seq

    def conv_work():
        for c in range(N_CHUNK):
            for sl in range(N_SLAB):
                prev = hcarry[sl] if c == 0 else zbuf[sl, (c - 1) * SPAN + CHUNK:c * SPAN, :]
                zbuf[sl, c * SPAN:c * SPAN + HIST, :] = jnp.where(
                    mix_is_prompt, prev, cci_ref[c, :, sl * LANES:(sl + 1) * LANES])
            _conv_chunk(c * SPAN + HIST, slice(c * CHUNK, (c + 1) * CHUNK), zbuf,
                        dw_ref, dwb_ref, lng_ref, lnb_ref, m_ref)
            for sl in range(N_SLAB):
                ccs_ref[c, :, sl * LANES:(sl + 1) * LANES] = zbuf[sl, c * SPAN + CHUNK:(c + 1) * SPAN, :]

    def state_in(c, s_prev):
        return jnp.where(mix_is_prompt, s_scr[...] if c == 0 else s_prev, si_ref[c])

    def store_state(c, s):
        ss_ref[c] = s
        if c == N_CHUNK - 1:
            s_next[...] = s

    def gla_work():
        return _gla_tile(state_in, store_state, qk_scr, v_scr, sg_scr, la_scr, gng_ref, m_ref)

    def x_tile():
        return jnp.where(ffn_is_prompt, xa_ref[...], xb_ref[...])

    def mixer_begin():
        @pl.when(mix_is_prompt & (seq_tile == 0))
        def _():
            hcarry[...] = jnp.zeros(hcarry.shape, F32)
            s_scr[...] = jnp.zeros(s_scr.shape, F32)

    def mixer_end(need_exact):
        @pl.when(need_exact)
        def _():
            _gla_tile_exact(state_in, store_state, qk_scr, v_scr, sg_scr, la_scr, gng_ref, m_ref, b_scr)

        s_scr[...] = s_next[...]
        hcarry[...] = zbuf[:, (N_CHUNK - 1) * SPAN + CHUNK:N_CHUNK * SPAN, :]

        @pl.when(mix_is_prompt & (seq_tile == tiles_per_seq - 1))
        def _():
            b = j // tiles_per_seq
            sp_ref[b] = s_scr[...]
            for sl in range(N_SLAB):
                ccp_ref[b, :, sl * LANES:(sl + 1) * LANES] = hcarry[sl]

    def gate_up_region(with_mixer):
        h_ref[...] = _rms(x_tile(), n1_ref[...]).astype(BF16)

        @pl.when(i >= 0)
        def _():
            gen = gla_work() if with_mixer else None
            _gate_up(h_ref[...], wg_ref, wu_ref, a_ref, 0, D_FF, gen)
            if with_mixer:
                for _ in gen:
                    pass
                conv_work()

    def ffn_tail():
        x1 = x_tile() + 0.5 * _dot(a_ref[...], wd_ref[...])
        x1_ref[...] = x1
        h2 = _rms(x1, n2_ref[...]).astype(BF16)
        ua = _dot(h2, win_ref[:, 0:C_CONV])
        ub = _dot(h2, win_ref[:, C_CONV:2 * C_CONV])
        z = ua * _sigmoid(ub)
        for c in range(N_CHUNK):
            for sl in range(N_SLAB):
                zbuf[sl, c * SPAN + HIST:(c + 1) * SPAN, :] = z[c * CHUNK:(c + 1) * CHUNK, sl * LANES:(sl + 1) * LANES]
        o = 2 * C_CONV
        qk_scr[:, 0:QK_W] = _dot(h2, win_ref[:, o:o + QK_W]) * (DK ** -0.5)
        qk_scr[:, QK_W:2 * QK_W] = _dot(h2, win_ref[:, o + QK_W:o + 2 * QK_W])
        o += 2 * QK_W
        v_scr[...] = _dot(h2, win_ref[:, o:o + V_W]).astype(BF16)
        g = _dot(h2, win_ref[:, o + V_W:o + 2 * V_W])
        sg_scr[...] = g * _sigmoid(g)
        pre = _dot(h2, wgate_ref[...]) + gb_ref[...]
        la_scr[...] = (jnp.minimum(pre, 0.0) - jnp.log(1.0 + jnp.exp(-jnp.abs(pre)))) * (1.0 / GATE_TAU)

    @pl.when(i == 0)
    def _():
        _load_weights(((wg_hbm.at[layer], wg_ref), (wu_hbm.at[layer], wu_ref), (wd_hbm.at[layer], wd_ref),
                       (win_hbm.at[layer, :, pl.ds(0, N_MAIN)], win_ref)), sem)
        wgate_ref[...] = jnp.dot(wgz_ref[...], w2_ref[...], preferred_element_type=F32,
                                 precision=lax.Precision.HIGHEST).astype(BF16)
        gate_up_region(False)
        ffn_tail()

    @pl.when((i > 0) & (i < n_tiles))
    def _():
        need_exact = jnp.min(la_scr[...]) < FAST_LOG_DECAY_MIN
        mixer_begin()
        gate_up_region(True)
        mixer_end(need_exact)
        ffn_tail()

    @pl.when(i == n_tiles)
    def _():
        need_exact = jnp.min(la_scr[...]) < FAST_LOG_DECAY_MIN
        mixer_begin()
        for _ in gla_work():
            pass
        conv_work()
        mixer_end(need_exact)


def _ffn_mixer(xa, xb, xb_tile0, cc_in, s_in, w, layer, n_prompt_tiles, n_sample_tiles, tiles_per_seq, n_prompt_seq):
    n_tiles = n_prompt_tiles + n_sample_tiles
    sample_tile = lambda t: jnp.clip(t - n_prompt_tiles, 0, n_sample_tiles - 1)
    row = lambda f: pl.BlockSpec((TM, D_MODEL), lambda i: (f(i), 0))
    per_seq = lambda r, c: pl.BlockSpec((N_CHUNK, r, c), lambda i: (sample_tile(i - 1), 0, 0))
    once = pl.Buffered(1)
    per_seq_in = lambda r, c: pl.BlockSpec((None, N_CHUNK, r, c), lambda i: (layer, sample_tile(i - 1), 0, 0),
                                           pipeline_mode=once)
    xb_spec = pl.BlockSpec((TM, D_MODEL), lambda i: (xb_tile0 + sample_tile(i), 0), pipeline_mode=once)
    n_sample_seq = n_sample_tiles * N_CHUNK
    return pl.pallas_call(
        functools.partial(_ffn_mixer_kernel, layer=layer, n_prompt_tiles=n_prompt_tiles, n_tiles=n_tiles,
                          tiles_per_seq=tiles_per_seq),
        grid=(n_tiles + 1,),
        in_specs=[row(lambda i: jnp.minimum(i, n_prompt_tiles - 1)), xb_spec]
        + [_HBM] * 4 + [_VMEM] * 10 + [per_seq_in(HIST, C_CONV), per_seq_in(QK_W, DV)],
        out_specs=[row(lambda i: jnp.minimum(i, n_tiles - 1)), row(lambda i: jnp.maximum(i - 1, 0)),
                   _VMEM, _VMEM, per_seq(HIST, C_CONV), per_seq(QK_W, DV)],
        out_shape=[jax.ShapeDtypeStruct((n_tiles * TM, D_MODEL), F32),
                   jax.ShapeDtypeStruct((n_tiles * TM, D_MODEL), BF16),
                   jax.ShapeDtypeStruct((n_prompt_seq, HIST, C_CONV), F32),
                   jax.ShapeDtypeStruct((n_prompt_seq, QK_W, DV), F32),
                   jax.ShapeDtypeStruct((n_sample_seq, HIST, C_CONV), F32),
                   jax.ShapeDtypeStruct((n_sample_seq, QK_W, DV), F32)],
        scratch_shapes=[pltpu.VMEM((D_MODEL, D_FF), BF16), pltpu.VMEM((D_MODEL, D_FF), BF16),
                        pltpu.VMEM((D_FF, D_MODEL), BF16), pltpu.VMEM((D_MODEL, N_MAIN), BF16),
                        pltpu.SemaphoreType.DMA((4,)),
                        pltpu.VMEM((TM, D_FF), BF16), pltpu.VMEM((TM, D_MODEL), BF16),
                        pltpu.VMEM((N_SLAB, N_CHUNK * SPAN, LANES), F32), pltpu.VMEM((N_SLAB, HIST, LANES), F32),
                        pltpu.VMEM((TM, 2 * QK_W), F32), pltpu.VMEM((TM, V_W), BF16),
                        pltpu.VMEM((TM, V_W), F32), pltpu.VMEM((TM, QK_W), F32),
                        pltpu.VMEM((QK_W, DV), F32), pltpu.VMEM((QK_W, DV), F32), pltpu.VMEM((CHUNK, QK_W), F32),
                        pltpu.VMEM((D_MODEL, QK_W), BF16)],
        compiler_params=pltpu.CompilerParams(dimension_semantics=("arbitrary",), vmem_limit_bytes=VMEM_LIMIT),
        name="ffn_mixer",
    )(xa, xb, w["wg1"], w["wu1"], w["wd1"], w["win"], w["n1"][layer], w["n2"][layer], w["wgz"][layer],
      w["w2"][layer], w["gb"][layer], w["dw"][layer], w["dwb"][layer], w["lng"][layer], w["lnb"][layer],
      w["gng"][layer], cc_in, s_in)


def _outproj_ffn_kernel(x1_ref, m_ref, wo_hbm, wg_hbm, wu_hbm, wd_hbm, n3_ref, nf_ref, *rest,
                        layer, final, n_prompt_tiles):
    if final:
        yp_ref, ys_ref, wo_ref, wg_ref, wu_ref, wd_ref, sem, a_ref = rest
    else:
        o_ref, wo_ref, wg_ref, wu_ref, wd_ref, sem, a_ref = rest
    i = pl.program_id(0)

    @pl.when(i == 0)
    def _():
        _load_weights(((wo_hbm.at[layer], wo_ref), (wg_hbm.at[layer], wg_ref), (wu_hbm.at[layer], wu_ref),
                       (wd_hbm.at[layer], wd_ref)), sem)

    x2 = x1_ref[...] + _dot(m_ref[...], wo_ref[...])
    h = _rms(x2, n3_ref[...]).astype(BF16)
    _gate_up(h, wg_ref, wu_ref, a_ref, 0, D_FF)
    x3 = x2 + 0.5 * _dot(a_ref[...], wd_ref[...])
    if final:
        y = _rms(x3, nf_ref[...])

        @pl.when(i < n_prompt_tiles)
        def _():
            yp_ref[...] = y

        @pl.when(i >= n_prompt_tiles)
        def _():
            ys_ref[...] = y
    else:
        o_ref[...] = x3


def _outproj_ffn(x1, mix, w, layer, final, n_prompt_tiles, n_sample_tiles):
    n_tiles = n_prompt_tiles + n_sample_tiles
    row = lambda f: pl.BlockSpec((TM, D_MODEL), lambda i: (f(i), 0))
    if final:
        out_specs = [row(lambda i: jnp.minimum(i, n_prompt_tiles - 1)),
                     row(lambda i: jnp.clip(i - n_prompt_tiles, 0, n_sample_tiles - 1))]
        out_shape = [jax.ShapeDtypeStruct((n_prompt_tiles * TM, D_MODEL), F32),
                     jax.ShapeDtypeStruct((n_sample_tiles * TM, D_MODEL), F32)]
    else:
        out_specs = row(lambda i: i)
        out_shape = jax.ShapeDtypeStruct((n_tiles * TM, D_MODEL), F32)
    return pl.pallas_call(
        functools.partial(_outproj_ffn_kernel, layer=layer, final=final, n_prompt_tiles=n_prompt_tiles),
        grid=(n_tiles,),
        in_specs=[row(lambda i: i), row(lambda i: i)] + [_HBM] * 4 + [_VMEM] * 2,
        out_specs=out_specs,
        out_shape=out_shape,
        scratch_shapes=[pltpu.VMEM((D_MODEL, D_MODEL), BF16), pltpu.VMEM((D_MODEL, D_FF), BF16),
                        pltpu.VMEM((D_MODEL, D_FF), BF16), pltpu.VMEM((D_FF, D_MODEL), BF16),
                        pltpu.SemaphoreType.DMA((4,)),
                        pltpu.VMEM((TM, D_FF), BF16)],
        compiler_params=pltpu.CompilerParams(dimension_semantics=("arbitrary",), vmem_limit_bytes=VMEM_LIMIT),
        name="outproj_ffn",
    )(x1, mix, w["wo"], w["wg2"], w["wu2"], w["wd2"], w["n3"][layer], w["nf"])


def kernel(x_prompt, x_sample, cache_conv, state_gla, ffn1_norm, ffn1_w_gate, ffn1_w_up, ffn1_w_down, mix_norm, w_in, conv_dw_w, conv_dw_b, conv_ln_g, conv_ln_b, gla_gate_w2, gla_gate_b, gla_norm_g, w_out, ffn2_norm, ffn2_w_gate, ffn2_w_up, ffn2_w_down, final_norm):
    bp, lp, _ = x_prompt.shape
    bs, ls, _ = x_sample.shape
    assert lp % TM == 0 and ls == CHUNK and (bs * ls) % TM == 0
    n_prompt_tiles, n_sample_tiles, tiles_per_seq = bp * lp // TM, bs * ls // TM, lp // TM

    layers = lambda f: [f(l) for l in range(DEPTH)]
    row = lambda a: layers(lambda l: a[l].reshape(1, a.shape[-1]))
    w = dict(
        wg1=ffn1_w_gate.astype(BF16), wu1=ffn1_w_up.astype(BF16), wd1=ffn1_w_down.astype(BF16),
        win=w_in.astype(BF16), wo=w_out.astype(BF16), wg2=ffn2_w_gate.astype(BF16),
        wu2=ffn2_w_up.astype(BF16), wd2=ffn2_w_down.astype(BF16), n1=row(ffn1_norm), n2=row(mix_norm), n3=row(ffn2_norm), nf=final_norm.reshape(1, D_MODEL),
        wgz=layers(lambda l: jnp.pad(w_in[l, :, N_MAIN:], ((0, 0), (0, LANES - GATE_RANK)))),
        w2=layers(lambda l: jnp.pad(gla_gate_w2[l], ((0, LANES - GATE_RANK), (0, 0)))), gb=row(gla_gate_b),
        dw=layers(lambda l: jnp.pad(conv_dw_w[l], ((0, 1), (0, 0)))), dwb=row(conv_dw_b), lng=row(conv_ln_g),
        lnb=row(conv_ln_b), gng=row(gla_norm_g),
    )
    cc_in = jnp.pad(cache_conv, ((0, 0), (0, 0), (HIST - (CONV_W - 1), 0), (0, 0)))
    s_in = state_gla.reshape(DEPTH, bs, QK_W, DV)

    xa, xb, xb_tile0 = x_prompt.reshape(bp * lp, D_MODEL), x_sample.reshape(bs * ls, D_MODEL), 0
    cc_p, s_p, cc_s, s_s = [], [], [], []
    for l in range(DEPTH):
        x1, mix, ccp, sp, ccs, ss = _ffn_mixer(xa, xb, xb_tile0, cc_in, s_in, w, l, n_prompt_tiles,
                                               n_sample_tiles, tiles_per_seq, bp)
        cc_p.append(ccp[:, HIST - (CONV_W - 1):, :])
        cc_s.append(ccs[:, HIST - (CONV_W - 1):, :])
        s_p.append(sp.reshape(bp, HEADS, DK, DV))
        s_s.append(ss.reshape(bs, HEADS, DK, DV))
        out = _outproj_ffn(x1, mix, w, l, l == DEPTH - 1, n_prompt_tiles, n_sample_tiles)
        xa, xb, xb_tile0 = out, out, n_prompt_tiles
    y_p, y_s = out
    return (y_p.reshape(bp, lp, D_MODEL), y_s.reshape(bs, ls, D_MODEL), jnp.stack(cc_p), jnp.stack(s_p),
            jnp.stack(cc_s), jnp.stack(s_s))
```

```python
import functools

import jax
import jax.numpy as jnp
from jax import lax
from jax.experimental import pallas as pl
from jax.experimental.pallas import tpu as pltpu

D_MODEL = 1024
DEPTH = 2
C_CONV = 512
CONV_W = 31
HEADS = 4
DK = 64
DV = 128
QK_W = HEADS * DK
V_W = HEADS * DV
N_MAIN = 2 * C_CONV + 2 * QK_W + 2 * V_W
GATE_RANK = 16
GATE_TAU = 16.0
D_FF = 2816
EPS = 1e-6
CHUNK = 64
SUB = 16
NSUB = CHUNK // SUB
FAST_LOG_DECAY_MIN = -4.0

LANES = 128
N_SLAB = C_CONV // LANES
HIST = 32
SPAN = HIST + CHUNK
FF_CHUNK = 256
TM = 512
N_CHUNK = TM // CHUNK
VMEM_LIMIT = 58 * 1024 * 1024

F32 = jnp.float32
BF16 = jnp.bfloat16

_VMEM = pl.BlockSpec(memory_space=pltpu.VMEM)
_HBM = pl.BlockSpec(memory_space=pl.ANY)


def _sigmoid(x):
    return 1.0 / (1.0 + jnp.exp(-x))


def _rms(x, g):
    return x * lax.rsqrt(jnp.mean(x * x, axis=-1, keepdims=True) + EPS) * g


def _dot(a, b):
    return jnp.dot(a, b, preferred_element_type=F32)


def _load_weights(pairs, sem):
    copies = [pltpu.make_async_copy(src, dst, sem.at[k]) for k, (src, dst) in enumerate(pairs)]
    for c in copies:
        c.start()
    for c in copies:
        c.wait()


def _gate_up(h, wg_ref, wu_ref, a_ref, lo, hi, between=None):
    for c in range(lo, hi, FF_CHUNK):
        if between is not None:
            next(between, None)
        sl = slice(c, c + FF_CHUNK)
        g = _dot(h, wg_ref[:, sl])
        u = _dot(h, wu_ref[:, sl])
        a_ref[:, sl] = (g * _sigmoid(g) * u).astype(BF16)


def _gla_cumsum(la):
    r = lax.broadcasted_iota(jnp.int32, (CHUNK, CHUNK), 0)
    c = lax.broadcasted_iota(jnp.int32, (CHUNK, CHUNK), 1)
    tril = (r >= c).astype(BF16)
    la_hi = la.astype(BF16)
    r1 = la - la_hi.astype(F32)
    la_mid = r1.astype(BF16)
    la_lo = (r1 - la_mid.astype(F32)).astype(BF16)
    return _dot(tril, la_hi) + _dot(tril, la_mid) + _dot(tril, la_lo)


def _gla_operands(q, k, b):
    lane_head = lax.broadcasted_iota(jnp.int32, (CHUNK, QK_W), 1) // DK
    row_head = lax.broadcasted_iota(jnp.int32, (CHUNK, QK_W), 0) // SUB
    head_mask = lane_head == row_head
    j_idx = lax.broadcasted_iota(jnp.int32, (CHUNK, QK_W), 0)
    b_last = b[CHUNK - 1:CHUNK, :]
    qe = q * jnp.exp(b)
    qxs, kxs, qe_blocks = [], [], []
    for i in range(NSUB):
        rows = slice(i * SUB, (i + 1) * SUB)
        b_ref = b[i * SUB - 1:i * SUB, :] if i > 0 else jnp.zeros((1, QK_W), F32)
        qt = q[rows] * jnp.exp(b[rows] - b_ref)
        qxs.append(jnp.where(head_mask, jnp.concatenate([qt] * HEADS, axis=0), 0.0).astype(BF16))
        kxs.append(jnp.where(j_idx < (i + 1) * SUB, k * jnp.exp(b_ref - b), 0.0).astype(BF16))
        qe_blocks.append(jnp.where(head_mask, jnp.concatenate([qe[rows]] * HEADS, axis=0), 0.0).astype(BF16))
    qe_all = jnp.concatenate(qe_blocks, axis=0)
    kd = (k * jnp.exp(b_last - b)).astype(BF16)
    decay = jnp.exp(jnp.broadcast_to(b_last, (DV, QK_W)).T)
    return qxs, kxs, qe_all, kd, decay


def _gla_scores(qxs, kxs):
    r = lax.broadcasted_iota(jnp.int32, (CHUNK, CHUNK), 0)
    c = lax.broadcasted_iota(jnp.int32, (CHUNK, CHUNK), 1)
    row_local = r % SUB
    p_blocks = []
    for i in range(NSUB):
        a = lax.dot_general(qxs[i], kxs[i], (((1,), (1,)), ((), ())), preferred_element_type=F32)
        p_blocks.append(jnp.where(c <= i * SUB + row_local, a, 0.0).astype(BF16))
    return jnp.concatenate(p_blocks, axis=0)


def _gla_update(kd, v):
    upd = lax.dot_general(kd, v, (((0,), (0,)), ((), ())), preferred_element_type=F32)
    return jnp.concatenate([upd[h * DK:(h + 1) * DK, h * DV:(h + 1) * DV] for h in range(HEADS)], axis=0)


def _gla_output(p_all, qe_all, v, s):
    intra = _dot(p_all, v)
    inter = _dot(qe_all, s.astype(BF16))
    o_rows = []
    for i in range(NSUB):
        blocks = []
        for h in range(HEADS):
            r0 = i * CHUNK + h * SUB
            blocks.append(intra[r0:r0 + SUB, h * DV:(h + 1) * DV] + inter[r0:r0 + SUB, :])
        o_rows.append(jnp.concatenate(blocks, axis=1))
    return jnp.concatenate(o_rows, axis=0)


def _gla_tile(state_in, store_state, qk_scr, v_scr, sg_scr, la_scr, gng_ref, m_ref):
    rows = [slice(c * CHUNK, (c + 1) * CHUNK) for c in range(N_CHUNK)]
    bs = [_gla_cumsum(la_scr[r, :]) for r in rows]
    yield
    ops = [_gla_operands(qk_scr[r, 0:QK_W], qk_scr[r, QK_W:2 * QK_W], b) for r, b in zip(rows, bs)]
    ps = [_gla_scores(o[0], o[1]) for o in ops]
    upds = [_gla_update(o[3], v_scr[r, :]) for r, o in zip(rows, ops)]
    yield
    s_in, s_prev = [], None
    for c in range(N_CHUNK):
        s_in.append(state_in(c, s_prev))
        s_prev = ops[c][4] * s_in[c] + upds[c]
        store_state(c, s_prev)
    outs = [_gla_output(ps[c], ops[c][2], v_scr[rows[c], :], s_in[c]) for c in range(N_CHUNK)]
    yield
    for c in range(N_CHUNK):
        for h in range(HEADS):
            hs = slice(h * DV, (h + 1) * DV)
            oh = outs[c][:, hs]
            on = oh * lax.rsqrt(jnp.mean(oh * oh, axis=-1, keepdims=True) + EPS) * gng_ref[:, hs]
            m_ref[rows[c], C_CONV + h * DV:C_CONV + (h + 1) * DV] = (on * sg_scr[rows[c], hs]).astype(BF16)
    yield


def _gla_scores_exact(q, k_row, b, b_scr):
    b_scr[...] = b
    lane_head = lax.broadcasted_iota(jnp.int32, (CHUNK, QK_W), 1) // DK
    row_head = lax.broadcasted_iota(jnp.int32, (CHUNK, QK_W), 0) // SUB
    head_mask = lane_head == row_head
    q_rows, b_rows = [], []
    for i in range(NSUB):
        rows = slice(i * SUB, (i + 1) * SUB)
        q_rows.append(jnp.where(head_mask, jnp.concatenate([q[rows]] * HEADS, axis=0), 0.0))
        b_rows.append(jnp.concatenate([b[rows]] * HEADS, axis=0))
    q_rows = jnp.concatenate(q_rows, axis=0)
    b_rows = jnp.concatenate(b_rows, axis=0)
    n_rows = NSUB * CHUNK
    col = lax.broadcasted_iota(jnp.int32, (n_rows, LANES), 1)
    r = lax.broadcasted_iota(jnp.int32, (n_rows, LANES), 0)
    token = r // CHUNK * SUB + r % SUB

    def body(j, acc):
        p = q_rows * k_row(j) * jnp.exp(jnp.minimum(b_rows - b_scr[pl.ds(j, 1), :], 0.0))
        return jnp.where(col == j, jnp.sum(p, axis=-1, keepdims=True), acc)

    acc = lax.fori_loop(0, CHUNK, body, jnp.zeros((n_rows, LANES), F32))
    return jnp.where(col <= token, acc, 0.0)[:, 0:CHUNK].astype(BF16)


def _gla_tile_exact(state_in, store_state, qk_scr, v_scr, sg_scr, la_scr, gng_ref, m_ref, b_scr):
    s_prev = None
    for c in range(N_CHUNK):
        rows = slice(c * CHUNK, (c + 1) * CHUNK)
        q = qk_scr[rows, 0:QK_W]
        b = _gla_cumsum(la_scr[rows, :])
        _, _, qe_all, kd, decay = _gla_operands(q, qk_scr[rows, QK_W:2 * QK_W], b)
        p_all = _gla_scores_exact(q, lambda j: qk_scr[pl.ds(c * CHUNK + j, 1), QK_W:2 * QK_W], b, b_scr)
        s_in = state_in(c, s_prev)
        s_prev = decay * s_in + _gla_update(kd, v_scr[rows, :])
        store_state(c, s_prev)
        o = _gla_output(p_all, qe_all, v_scr[rows, :], s_in)
        for h in range(HEADS):
            hs = slice(h * DV, (h + 1) * DV)
            oh = o[:, hs]
            on = oh * lax.rsqrt(jnp.mean(oh * oh, axis=-1, keepdims=True) + EPS) * gng_ref[:, hs]
            m_ref[rows, C_CONV + h * DV:C_CONV + (h + 1) * DV] = (on * sg_scr[rows, hs]).astype(BF16)


def _conv_chunk(zrow, rows, zbuf, dw_ref, dwb_ref, lng_ref, lnb_ref, m_ref):
    acc = []
    for j in range(N_SLAB):
        ls = slice(j * LANES, (j + 1) * LANES)
        a = jnp.broadcast_to(dwb_ref[:, ls], (CHUNK, LANES))
        for w in range(CONV_W):
            a = a + zbuf[j, pl.ds(zrow - (CONV_W - 1) + w, CHUNK, stride=1), :] * dw_ref[w:w + 1, ls]
        acc.append(a)
    mu = sum(jnp.sum(a, axis=-1, keepdims=True) for a in acc) * (1.0 / C_CONV)
    xc = [a - mu for a in acc]
    rstd = lax.rsqrt(sum(jnp.sum(x * x, axis=-1, keepdims=True) for x in xc) * (1.0 / C_CONV) + EPS)
    for j in range(N_SLAB):
        ls = slice(j * LANES, (j + 1) * LANES)
        yn = xc[j] * rstd * lng_ref[:, ls] + lnb_ref[:, ls]
        m_ref[rows, ls] = (yn * _sigmoid(yn)).astype(BF16)


def _ffn_mixer_kernel(xa_ref, xb_ref, wg_hbm, wu_hbm, wd_hbm, win_hbm,
                      n1_ref, n2_ref, wgz_ref, w2_ref, gb_ref, dw_ref, dwb_ref, lng_ref, lnb_ref, gng_ref,
                      cci_ref, si_ref,
                      x1_ref, m_ref, ccp_ref, sp_ref, ccs_ref, ss_ref,
                      wg_ref, wu_ref, wd_ref, win_ref, sem, a_ref, h_ref, zbuf, hcarry,
                      qk_scr, v_scr, sg_scr, la_scr, s_scr, s_next, b_scr, wgate_ref,
                      *, layer, n_prompt_tiles, tiles_per_seq):
    i = pl.program_id(0)
    j = jnp.maximum(i - 1, 0)
    ffn_is_prompt = i < n_prompt_tiles
    mix_is_prompt = j < n_prompt_tiles
    seq_tile = j % tiles_per_seq

    def conv_work():
        for c in range(N_CHUNK):
            for sl in range(N_SLAB):
                prev = hcarry[sl] if c == 0 else zbuf[sl, (c - 1) * SPAN + CHUNK:c * SPAN, :]
                zbuf[sl, c * SPAN:c * SPAN + HIST, :] = jnp.where(
                    mix_is_prompt, prev, cci_ref[c, :, sl * LANES:(sl + 1) * LANES])
            _conv_chunk(c * SPAN + HIST, slice(c * CHUNK, (c + 1) * CHUNK), zbuf,
                        dw_ref, dwb_ref, lng_ref, lnb_ref, m_ref)
            for sl in range(N_SLAB):
                ccs_ref[c, :, sl * LANES:(sl + 1) * LANES] = zbuf[sl, c * SPAN + CHUNK:(c + 1) * SPAN, :]

    def state_in(c, s_prev):
        return jnp.where(mix_is_prompt, s_scr[...] if c == 0 else s_prev, si_ref[c])

    def store_state(c, s):
        ss_ref[c] = s
        if c == N_CHUNK - 1:
            s_next[...] = s

    def gla_work():
        return _gla_tile(state_in, store_state, qk_scr, v_scr, sg_scr, la_scr, gng_ref, m_ref)

    def x_tile():
        return jnp.where(ffn_is_prompt, xa_ref[...], xb_ref[...])

    def mixer_begin():
        @pl.when(mix_is_prompt & (seq_tile == 0))
        def _():
            hcarry[...] = jnp.zeros(hcarry.shape, F32)
            s_scr[...] = jnp.zeros(s_scr.shape, F32)

    def mixer_end(need_exact):
        @pl.when(need_exact)
        def _():
            _gla_tile_exact(state_in, store_state, qk_scr, v_scr, sg_scr, la_scr, gng_ref, m_ref, b_scr)

        s_scr[...] = s_next[...]
        hcarry[...] = zbuf[:, (N_CHUNK - 1) * SPAN + CHUNK:N_CHUNK * SPAN, :]

        @pl.when(mix_is_prompt & (seq_tile == tiles_per_seq - 1))
        def _():
            b = j // tiles_per_seq
            sp_ref[b] = s_scr[...]
            for sl in range(N_SLAB):
                ccp_ref[b, :, sl * LANES:(sl + 1) * LANES] = hcarry[sl]

    def gate_up_region():
        h_ref[...] = _rms(x_tile(), n1_ref[...]).astype(BF16)

        @pl.when(i >= 0)
        def _():
            gen = gla_work()
            _gate_up(h_ref[...], wg_ref, wu_ref, a_ref, 0, D_FF, gen)
            for _ in gen:
                pass
            conv_work()

    def ffn_tail():
        x1 = x_tile() + 0.5 * _dot(a_ref[...], wd_ref[...])
        x1_ref[...] = x1
        h2 = _rms(x1, n2_ref[...]).astype(BF16)
        ua = _dot(h2, win_ref[:, 0:C_CONV])
        ub = _dot(h2, win_ref[:, C_CONV:2 * C_CONV])
        z = ua * _sigmoid(ub)
        for c in range(N_CHUNK):
            for sl in range(N_SLAB):
                zbuf[sl, c * SPAN + HIST:(c + 1) * SPAN, :] = z[c * CHUNK:(c + 1) * CHUNK, sl * LANES:(sl + 1) * LANES]
        o = 2 * C_CONV
        qk_scr[:, 0:QK_W] = _dot(h2, win_ref[:, o:o + QK_W]) * (DK ** -0.5)
        qk_scr[:, QK_W:2 * QK_W] = _dot(h2, win_ref[:, o + QK_W:o + 2 * QK_W])
        o += 2 * QK_W
        v_scr[...] = _dot(h2, win_ref[:, o:o + V_W]).astype(BF16)
        g = _dot(h2, win_ref[:, o + V_W:o + 2 * V_W])
        sg_scr[...] = g * _sigmoid(g)
        pre = _dot(h2, wgate_ref[...]) + gb_ref[...]
        la_scr[...] = (jnp.minimum(pre, 0.0) - jnp.log(1.0 + jnp.exp(-jnp.abs(pre)))) * (1.0 / GATE_TAU)

    @pl.when(i == 0)
    def _():
        for ref in (zbuf, hcarry, qk_scr, v_scr, sg_scr, la_scr, s_scr, s_next):
            ref[...] = jnp.zeros(ref.shape, ref.dtype)
        _load_weights(((wg_hbm.at[layer], wg_ref), (wu_hbm.at[layer], wu_ref), (wd_hbm.at[layer], wd_ref),
                       (win_hbm.at[layer, :, pl.ds(0, N_MAIN)], win_ref)), sem)
        wgate_ref[...] = jnp.dot(wgz_ref[...], w2_ref[...], preferred_element_type=F32,
                                 precision=lax.Precision.HIGHEST).astype(BF16)

    need_exact = jnp.min(la_scr[...]) < FAST_LOG_DECAY_MIN
    mixer_begin()
    gate_up_region()
    mixer_end(need_exact)
    ffn_tail()


def _ffn_mixer(xa, xb, xb_tile0, cc_in, s_in, w, layer, n_prompt_tiles, n_sample_tiles, tiles_per_seq, n_prompt_seq):
    n_tiles = n_prompt_tiles + n_sample_tiles
    sample_tile = lambda t: jnp.clip(t - n_prompt_tiles, 0, n_sample_tiles - 1)
    row = lambda f: pl.BlockSpec((TM, D_MODEL), lambda i: (f(i), 0))
    per_seq = lambda r, c: pl.BlockSpec((N_CHUNK, r, c), lambda i: (sample_tile(i - 1), 0, 0))
    once = pl.Buffered(1)
    per_seq_in = lambda r, c: pl.BlockSpec((None, N_CHUNK, r, c), lambda i: (layer, sample_tile(i - 1), 0, 0),
                                           pipeline_mode=once)
    xb_spec = pl.BlockSpec((TM, D_MODEL), lambda i: (xb_tile0 + sample_tile(i), 0), pipeline_mode=once)
    n_sample_seq = n_sample_tiles * N_CHUNK
    return pl.pallas_call(
        functools.partial(_ffn_mixer_kernel, layer=layer, n_prompt_tiles=n_prompt_tiles, tiles_per_seq=tiles_per_seq),
        grid=(n_tiles + 1,),
        in_specs=[row(lambda i: jnp.minimum(i, n_prompt_tiles - 1)), xb_spec]
        + [_HBM] * 4 + [_VMEM] * 10 + [per_seq_in(HIST, C_CONV), per_seq_in(QK_W, DV)],
        out_specs=[row(lambda i: jnp.minimum(i, n_tiles - 1)), row(lambda i: jnp.maximum(i - 1, 0)),
                   _VMEM, _VMEM, per_seq(HIST, C_CONV), per_seq(QK_W, DV)],
        out_shape=[jax.ShapeDtypeStruct((n_tiles * TM, D_MODEL), F32),
                   jax.ShapeDtypeStruct((n_tiles * TM, D_MODEL), BF16),
                   jax.ShapeDtypeStruct((n_prompt_seq, HIST, C_CONV), F32),
                   jax.ShapeDtypeStruct((n_prompt_seq, QK_W, DV), F32),
                   jax.ShapeDtypeStruct((n_sample_seq, HIST, C_CONV), F32),
                   jax.ShapeDtypeStruct((n_sample_seq, QK_W, DV), F32)],
        scratch_shapes=[pltpu.VMEM((D_MODEL, D_FF), BF16), pltpu.VMEM((D_MODEL, D_FF), BF16),
                        pltpu.VMEM((D_FF, D_MODEL), BF16), pltpu.VMEM((D_MODEL, N_MAIN), BF16),
                        pltpu.SemaphoreType.DMA((4,)),
                        pltpu.VMEM((TM, D_FF), BF16), pltpu.VMEM((TM, D_MODEL), BF16),
                        pltpu.VMEM((N_SLAB, N_CHUNK * SPAN, LANES), F32), pltpu.VMEM((N_SLAB, HIST, LANES), F32),
                        pltpu.VMEM((TM, 2 * QK_W), F32), pltpu.VMEM((TM, V_W), BF16),
                        pltpu.VMEM((TM, V_W), F32), pltpu.VMEM((TM, QK_W), F32),
                        pltpu.VMEM((QK_W, DV), F32), pltpu.VMEM((QK_W, DV), F32), pltpu.VMEM((CHUNK, QK_W), F32),
                        pltpu.VMEM((D_MODEL, QK_W), BF16)],
        compiler_params=pltpu.CompilerParams(dimension_semantics=("arbitrary",), vmem_limit_bytes=VMEM_LIMIT),
        name="ffn_mixer",
    )(xa, xb, w["wg1"], w["wu1"], w["wd1"], w["win"], w["n1"][layer], w["n2"][layer], w["wgz"][layer],
      w["w2"][layer], w["gb"][layer], w["dw"][layer], w["dwb"][layer], w["lng"][layer], w["lnb"][layer],
      w["gng"][layer], cc_in, s_in)


def _outproj_ffn_kernel(x1_ref, m_ref, wo_hbm, wg_hbm, wu_hbm, wd_hbm, n3_ref, nf_ref, *rest,
                        layer, final, n_prompt_tiles):
    if final:
        yp_ref, ys_ref, wo_ref, wg_ref, wu_ref, wd_ref, sem, a_ref = rest
    else:
        o_ref, wo_ref, wg_ref, wu_ref, wd_ref, sem, a_ref = rest
    i = pl.program_id(0)

    @pl.when(i == 0)
    def _():
        _load_weights(((wo_hbm.at[layer], wo_ref), (wg_hbm.at[layer], wg_ref), (wu_hbm.at[layer], wu_ref),
                       (wd_hbm.at[layer], wd_ref)), sem)

    x2 = x1_ref[...] + _dot(m_ref[...], wo_ref[...])
    h = _rms(x2, n3_ref[...]).astype(BF16)
    _gate_up(h, wg_ref, wu_ref, a_ref, 0, D_FF)
    x3 = x2 + 0.5 * _dot(a_ref[...], wd_ref[...])
    if final:
        y = _rms(x3, nf_ref[...])

        @pl.when(i < n_prompt_tiles)
        def _():
            yp_ref[...] = y

        @pl.when(i >= n_prompt_tiles)
        def _():
            ys_ref[...] = y
    else:
        o_ref[...] = x3


def _outproj_ffn(x1, mix, w, layer, final, n_prompt_tiles, n_sample_tiles):
    n_tiles = n_prompt_tiles + n_sample_tiles
    row = lambda f: pl.BlockSpec((TM, D_MODEL), lambda i: (f(i), 0))
    if final:
        out_specs = [row(lambda i: jnp.minimum(i, n_prompt_tiles - 1)),
                     row(lambda i: jnp.clip(i - n_prompt_tiles, 0, n_sample_tiles - 1))]
        out_shape = [jax.ShapeDtypeStruct((n_prompt_tiles * TM, D_MODEL), F32),
                     jax.ShapeDtypeStruct((n_sample_tiles * TM, D_MODEL), F32)]
    else:
        out_specs = row(lambda i: i)
        out_shape = jax.ShapeDtypeStruct((n_tiles * TM, D_MODEL), F32)
    return pl.pallas_call(
        functools.partial(_outproj_ffn_kernel, layer=layer, final=final, n_prompt_tiles=n_prompt_tiles),
        grid=(n_tiles,),
        in_specs=[row(lambda i: i), row(lambda i: i)] + [_HBM] * 4 + [_VMEM] * 2,
        out_specs=out_specs,
        out_shape=out_shape,
        scratch_shapes=[pltpu.VMEM((D_MODEL, D_MODEL), BF16), pltpu.VMEM((D_MODEL, D_FF), BF16),
                        pltpu.VMEM((D_MODEL, D_FF), BF16), pltpu.VMEM((D_FF, D_MODEL), BF16),
                        pltpu.SemaphoreType.DMA((4,)),
                        pltpu.VMEM((TM, D_FF), BF16)],
        compiler_params=pltpu.CompilerParams(dimension_semantics=("arbitrary",), vmem_limit_bytes=VMEM_LIMIT),
        name="outproj_ffn",
    )(x1, mix, w["wo"], w["wg2"], w["wu2"], w["wd2"], w["n3"][layer], w["nf"])


def kernel(x_prompt, x_sample, cache_conv, state_gla, ffn1_norm, ffn1_w_gate, ffn1_w_up, ffn1_w_down, mix_norm, w_in, conv_dw_w, conv_dw_b, conv_ln_g, conv_ln_b, gla_gate_w2, gla_gate_b, gla_norm_g, w_out, ffn2_norm, ffn2_w_gate, ffn2_w_up, ffn2_w_down, final_norm):
    bp, lp, _ = x_prompt.shape
    bs, ls, _ = x_sample.shape
    assert lp % TM == 0 and ls == CHUNK and (bs * ls) % TM == 0
    n_prompt_tiles, n_sample_tiles, tiles_per_seq = bp * lp // TM, bs * ls // TM, lp // TM

    layers = lambda f: [f(l) for l in range(DEPTH)]
    row = lambda a: layers(lambda l: a[l].reshape(1, a.shape[-1]))
    w = dict(
        wg1=ffn1_w_gate.astype(BF16), wu1=ffn1_w_up.astype(BF16), wd1=ffn1_w_down.astype(BF16),
        win=w_in.astype(BF16), wo=w_out.astype(BF16), wg2=ffn2_w_gate.astype(BF16),
        wu2=ffn2_w_up.astype(BF16), wd2=ffn2_w_down.astype(BF16), n1=row(ffn1_norm), n2=row(mix_norm), n3=row(ffn2_norm), nf=final_norm.reshape(1, D_MODEL),
        wgz=layers(lambda l: jnp.pad(w_in[l, :, N_MAIN:], ((0, 0), (0, LANES - GATE_RANK)))),
        w2=layers(lambda l: jnp.pad(gla_gate_w2[l], ((0, LANES - GATE_RANK), (0, 0)))), gb=row(gla_gate_b),
        dw=layers(lambda l: jnp.pad(conv_dw_w[l], ((0, 1), (0, 0)))), dwb=row(conv_dw_b), lng=row(conv_ln_g),
        lnb=row(conv_ln_b), gng=row(gla_norm_g),
    )
    cc_in = jnp.pad(cache_conv, ((0, 0), (0, 0), (HIST - (CONV_W - 1), 0), (0, 0)))
    s_in = state_gla.reshape(DEPTH, bs, QK_W, DV)

    xa, xb, xb_tile0 = x_prompt.reshape(bp * lp, D_MODEL), x_sample.reshape(bs * ls, D_MODEL), 0
    cc_p, s_p, cc_s, s_s = [], [], [], []
    for l in range(DEPTH):
        x1, mix, ccp, sp, ccs, ss = _ffn_mixer(xa, xb, xb_tile0, cc_in, s_in, w, l, n_prompt_tiles,
                                               n_sample_tiles, tiles_per_seq, bp)
        cc_p.append(ccp[:, HIST - (CONV_W - 1):, :])
        cc_s.append(ccs[:, HIST - (CONV_W - 1):, :])
        s_p.append(sp.reshape(bp, HEADS, DK, DV))
        s_s.append(ss.reshape(bs, HEADS, DK, DV))
        out = _outproj_ffn(x1, mix, w, l, l == DEPTH - 1, n_prompt_tiles, n_sample_tiles)
        xa, xb, xb_tile0 = out, out, n_prompt_tiles
    y_p, y_s = out
    return (y_p.reshape(bp, lp, D_MODEL), y_s.reshape(bs, ls, D_MODEL), jnp.stack(cc_p), jnp.stack(s_p),
            jnp.stack(cc_s), jnp.stack(s_s))
```

```python
import functools

import jax
import jax.numpy as jnp
from jax import lax
from jax.experimental import pallas as pl
from jax.experimental.pallas import tpu as pltpu

D_MODEL = 1024
DEPTH = 2
C_CONV = 512
CONV_W = 31
HEADS = 4
DK = 64
DV = 128
QK_W = HEADS * DK
V_W = HEADS * DV
N_MAIN = 2 * C_CONV + 2 * QK_W + 2 * V_W
GATE_RANK = 16
GATE_TAU = 16.0
D_FF = 2816
EPS = 1e-6
CHUNK = 64
SUB = 16
NSUB = CHUNK // SUB
FAST_LOG_DECAY_MIN = -4.0

LANES = 128
N_SLAB = C_CONV // LANES
HIST = 32
SPAN = HIST + CHUNK
FF_CHUNK = 256
TM = 512
N_CHUNK = TM // CHUNK
VMEM_LIMIT = 58 * 1024 * 1024

F32 = jnp.float32
BF16 = jnp.bfloat16

_VMEM = pl.BlockSpec(memory_space=pltpu.VMEM)
_HBM = pl.BlockSpec(memory_space=pl.ANY)


def _sigmoid(x):
    return 1.0 / (1.0 + jnp.exp(-x))


def _rms(x, g):
    return x * lax.rsqrt(jnp.mean(x * x, axis=-1, keepdims=True) + EPS) * g


def _dot(a, b):
    return jnp.dot(a, b, preferred_element_type=F32)


def _weight_copies(pairs, sem):
    return [pltpu.make_async_copy(src, dst, sem.at[k]) for k, (src, dst) in enumerate(pairs)]


def _gate_up(h, wg_ref, wu_ref, a_ref, lo, hi, between=None):
    for c in range(lo, hi, FF_CHUNK):
        if between is not None:
            next(between, None)
        sl = slice(c, c + FF_CHUNK)
        g = _dot(h, wg_ref[:, sl])
        u = _dot(h, wu_ref[:, sl])
        a_ref[:, sl] = (g * _sigmoid(g) * u).astype(BF16)


def _gla_cumsum(la):
    r = lax.broadcasted_iota(jnp.int32, (CHUNK, CHUNK), 0)
    c = lax.broadcasted_iota(jnp.int32, (CHUNK, CHUNK), 1)
    tril = (r >= c).astype(BF16)
    la_hi = la.astype(BF16)
    r1 = la - la_hi.astype(F32)
    la_mid = r1.astype(BF16)
    la_lo = (r1 - la_mid.astype(F32)).astype(BF16)
    return _dot(tril, la_hi) + _dot(tril, la_mid) + _dot(tril, la_lo)


def _gla_operands(q, k, b):
    lane_head = lax.broadcasted_iota(jnp.int32, (CHUNK, QK_W), 1) // DK
    row_head = lax.broadcasted_iota(jnp.int32, (CHUNK, QK_W), 0) // SUB
    head_mask = lane_head == row_head
    j_idx = lax.broadcasted_iota(jnp.int32, (CHUNK, QK_W), 0)
    b_last = b[CHUNK - 1:CHUNK, :]
    qe = q * jnp.exp(b)
    qxs, kxs, qe_blocks = [], [], []
    for i in range(NSUB):
        rows = slice(i * SUB, (i + 1) * SUB)
        b_ref = b[i * SUB - 1:i * SUB, :] if i > 0 else jnp.zeros((1, QK_W), F32)
        qt = q[rows] * jnp.exp(b[rows] - b_ref)
        qxs.append(jnp.where(head_mask, jnp.concatenate([qt] * HEADS, axis=0), 0.0).astype(BF16))
        kxs.append(jnp.where(j_idx < (i + 1) * SUB, k * jnp.exp(b_ref - b), 0.0).astype(BF16))
        qe_blocks.append(jnp.where(head_mask, jnp.concatenate([qe[rows]] * HEADS, axis=0), 0.0).astype(BF16))
    qe_all = jnp.concatenate(qe_blocks, axis=0)
    kd = (k * jnp.exp(b_last - b)).astype(BF16)
    decay = jnp.exp(jnp.broadcast_to(b_last, (DV, QK_W)).T)
    return qxs, kxs, qe_all, kd, decay


def _gla_scores(qxs, kxs):
    r = lax.broadcasted_iota(jnp.int32, (CHUNK, CHUNK), 0)
    c = lax.broadcasted_iota(jnp.int32, (CHUNK, CHUNK), 1)
    row_local = r % SUB
    p_blocks = []
    for i in range(NSUB):
        a = lax.dot_general(qxs[i], kxs[i], (((1,), (1,)), ((), ())), preferred_element_type=F32)
        p_blocks.append(jnp.where(c <= i * SUB + row_local, a, 0.0).astype(BF16))
    return jnp.concatenate(p_blocks, axis=0)


def _gla_update(kd, v):
    upd = lax.dot_general(kd, v, (((0,), (0,)), ((), ())), preferred_element_type=F32)
    return jnp.concatenate([upd[h * DK:(h + 1) * DK, h * DV:(h + 1) * DV] for h in range(HEADS)], axis=0)


def _gla_output(p_all, qe_all, v, s):
    intra = _dot(p_all, v)
    inter = _dot(qe_all, s.astype(BF16))
    o_rows = []
    for i in range(NSUB):
        blocks = []
        for h in range(HEADS):
            r0 = i * CHUNK + h * SUB
            blocks.append(intra[r0:r0 + SUB, h * DV:(h + 1) * DV] + inter[r0:r0 + SUB, :])
        o_rows.append(jnp.concatenate(blocks, axis=1))
    return jnp.concatenate(o_rows, axis=0)


def _gla_tile(state_in, store_state, qk_scr, v_scr, sg_scr, la_scr, gng_ref, m_ref):
    rows = [slice(c * CHUNK, (c + 1) * CHUNK) for c in range(N_CHUNK)]
    bs = [_gla_cumsum(la_scr[r, :]) for r in rows]
    yield
    ops = [_gla_operands(qk_scr[r, 0:QK_W], qk_scr[r, QK_W:2 * QK_W], b) for r, b in zip(rows, bs)]
    ps = [_gla_scores(o[0], o[1]) for o in ops]
    upds = [_gla_update(o[3], v_scr[r, :]) for r, o in zip(rows, ops)]
    yield
    s_in, s_prev = [], None
    for c in range(N_CHUNK):
        s_in.append(state_in(c, s_prev))
        s_prev = ops[c][4] * s_in[c] + upds[c]
        store_state(c, s_prev)
    outs = [_gla_output(ps[c], ops[c][2], v_scr[rows[c], :], s_in[c]) for c in range(N_CHUNK)]
    yield
    for c in range(N_CHUNK):
        for h in range(HEADS):
            hs = slice(h * DV, (h + 1) * DV)
            oh = outs[c][:, hs]
            on = oh * lax.rsqrt(jnp.mean(oh * oh, axis=-1, keepdims=True) + EPS) * gng_ref[:, hs]
            m_ref[rows[c], C_CONV + h * DV:C_CONV + (h + 1) * DV] = (on * sg_scr[rows[c], hs]).astype(BF16)
    yield


def _gla_scores_exact(q, k_row, b, b_scr):
    b_scr[...] = b
    lane_head = lax.broadcasted_iota(jnp.int32, (CHUNK, QK_W), 1) // DK
    row_head = lax.broadcasted_iota(jnp.int32, (CHUNK, QK_W), 0) // SUB
    head_mask = lane_head == row_head
    q_rows, b_rows = [], []
    for i in range(NSUB):
        rows = slice(i * SUB, (i + 1) * SUB)
        q_rows.append(jnp.where(head_mask, jnp.concatenate([q[rows]] * HEADS, axis=0), 0.0))
        b_rows.append(jnp.concatenate([b[rows]] * HEADS, axis=0))
    q_rows = jnp.concatenate(q_rows, axis=0)
    b_rows = jnp.concatenate(b_rows, axis=0)
    n_rows = NSUB * CHUNK
    col = lax.broadcasted_iota(jnp.int32, (n_rows, LANES), 1)
    r = lax.broadcasted_iota(jnp.int32, (n_rows, LANES), 0)
    token = r // CHUNK * SUB + r % SUB

    def body(j, acc):
        p = q_rows * k_row(j) * jnp.exp(jnp.minimum(b_rows - b_scr[pl.ds(j, 1), :], 0.0))
        return jnp.where(col == j, jnp.sum(p, axis=-1, keepdims=True), acc)

    acc = lax.fori_loop(0, CHUNK, body, jnp.zeros((n_rows, LANES), F32))
    return jnp.where(col <= token, acc, 0.0)[:, 0:CHUNK].astype(BF16)


def _gla_tile_exact(state_in, store_state, qk_scr, v_scr, sg_scr, la_scr, gng_ref, m_ref, b_scr):
    s_prev = None
    for c in range(N_CHUNK):
        rows = slice(c * CHUNK, (c + 1) * CHUNK)
        q = qk_scr[rows, 0:QK_W]
        b = _gla_cumsum(la_scr[rows, :])
        _, _, qe_all, kd, decay = _gla_operands(q, qk_scr[rows, QK_W:2 * QK_W], b)
        p_all = _gla_scores_exact(q, lambda j: qk_scr[pl.ds(c * CHUNK + j, 1), QK_W:2 * QK_W], b, b_scr)
        s_in = state_in(c, s_prev)
        s_prev = decay * s_in + _gla_update(kd, v_scr[rows, :])
        store_state(c, s_prev)
        o = _gla_output(p_all, qe_all, v_scr[rows, :], s_in)
        for h in range(HEADS):
            hs = slice(h * DV, (h + 1) * DV)
            oh = o[:, hs]
            on = oh * lax.rsqrt(jnp.mean(oh * oh, axis=-1, keepdims=True) + EPS) * gng_ref[:, hs]
            m_ref[rows, C_CONV + h * DV:C_CONV + (h + 1) * DV] = (on * sg_scr[rows, hs]).astype(BF16)


def _conv_chunk(zrow, rows, zbuf, dw_ref, dwb_ref, lng_ref, lnb_ref, m_ref):
    acc = []
    for j in range(N_SLAB):
        ls = slice(j * LANES, (j + 1) * LANES)
        a = jnp.broadcast_to(dwb_ref[:, ls], (CHUNK, LANES))
        for w in range(CONV_W):
            a = a + zbuf[j, pl.ds(zrow - (CONV_W - 1) + w, CHUNK, stride=1), :] * dw_ref[w:w + 1, ls]
        acc.append(a)
    mu = sum(jnp.sum(a, axis=-1, keepdims=True) for a in acc) * (1.0 / C_CONV)
    xc = [a - mu for a in acc]
    rstd = lax.rsqrt(sum(jnp.sum(x * x, axis=-1, keepdims=True) for x in xc) * (1.0 / C_CONV) + EPS)
    for j in range(N_SLAB):
        ls = slice(j * LANES, (j + 1) * LANES)
        yn = xc[j] * rstd * lng_ref[:, ls] + lnb_ref[:, ls]
        m_ref[rows, ls] = (yn * _sigmoid(yn)).astype(BF16)


def _ffn_mixer_kernel(xa_ref, xb_ref, wg_hbm, wu_hbm, wd_hbm, win_hbm,
                      n1_ref, n2_ref, wgz_ref, w2_ref, gb_ref, dw_ref, dwb_ref, lng_ref, lnb_ref, gng_ref,
                      cci_ref, si_ref,
                      x1_ref, m_ref, ccp_ref, sp_ref, ccs_ref, ss_ref,
                      wg_ref, wu_ref, wd_ref, win_ref, sem, a_ref, h_ref, zbuf, hcarry,
                      qk_scr, v_scr, sg_scr, la_scr, s_scr, s_next, b_scr, wgate_ref,
                      *, layer, n_prompt_tiles, n_tiles, tiles_per_seq):
    i = pl.program_id(0)
    j = jnp.maximum(i - 1, 0)
    ffn_is_prompt = i < n_prompt_tiles
    mix_is_prompt = j < n_prompt_tiles
    seq_tile = j % tiles_per_seq

    def conv_work():
        for c in range(N_CHUNK):
            for sl in range(N_SLAB):
                prev = hcarry[sl] if c == 0 else zbuf[sl, (c - 1) * SPAN + CHUNK:c * SPAN, :]
                zbuf[sl, c * SPAN:c * SPAN + HIST, :] = jnp.where(
                    mix_is_prompt, prev, cci_ref[c, :, sl * LANES:(sl + 1) * LANES])
            _conv_chunk(c * SPAN + HIST, slice(c * CHUNK, (c + 1) * CHUNK), zbuf,
                        dw_ref, dwb_ref, lng_ref, lnb_ref, m_ref)
            for sl in range(N_SLAB):
                ccs_ref[c, :, sl * LANES:(sl + 1) * LANES] = zbuf[sl, c * SPAN + CHUNK:(c + 1) * SPAN, :]

    def state_in(c, s_prev):
        return jnp.where(mix_is_prompt, s_scr[...] if c == 0 else s_prev, si_ref[c])

    def store_state(c, s):
        ss_ref[c] = s
        if c == N_CHUNK - 1:
            s_next[...] = s

    def gla_work():
        return _gla_tile(state_in, store_state, qk_scr, v_scr, sg_scr, la_scr, gng_ref, m_ref)

    def x_tile():
        return jnp.where(ffn_is_prompt, xa_ref[...], xb_ref[...])

    def mixer_begin():
        @pl.when(mix_is_prompt & (seq_tile == 0))
        def _():
            hcarry[...] = jnp.zeros(hcarry.shape, F32)
            s_scr[...] = jnp.zeros(s_scr.shape, F32)

    def mixer_end(need_exact):
        @pl.when(need_exact)
        def _():
            _gla_tile_exact(state_in, store_state, qk_scr, v_scr, sg_scr, la_scr, gng_ref, m_ref, b_scr)

        s_scr[...] = s_next[...]
        hcarry[...] = zbuf[:, (N_CHUNK - 1) * SPAN + CHUNK:N_CHUNK * SPAN, :]

        @pl.when(mix_is_prompt & (seq_tile == tiles_per_seq - 1))
        def _():
            b = j // tiles_per_seq
            sp_ref[b] = s_scr[...]
            for sl in range(N_SLAB):
                ccp_ref[b, :, sl * LANES:(sl + 1) * LANES] = hcarry[sl]

    def gate_up_region():
        h_ref[...] = _rms(x_tile(), n1_ref[...]).astype(BF16)

        @pl.when(i >= 0)
        def _():
            gen = gla_work()
            _gate_up(h_ref[...], wg_ref, wu_ref, a_ref, 0, D_FF, gen)
            for _ in gen:
                pass
            conv_work()

    def ffn_tail():
        x1 = x_tile() + 0.5 * _dot(a_ref[...], wd_ref[...])
        x1_ref[...] = x1
        h2 = _rms(x1, n2_ref[...]).astype(BF16)
        ua = _dot(h2, win_ref[:, 0:C_CONV])
        ub = _dot(h2, win_ref[:, C_CONV:2 * C_CONV])
        z = ua * _sigmoid(ub)
        for c in range(N_CHUNK):
            for sl in range(N_SLAB):
                zbuf[sl, c * SPAN + HIST:(c + 1) * SPAN, :] = z[c * CHUNK:(c + 1) * CHUNK, sl * LANES:(sl + 1) * LANES]
        o = 2 * C_CONV
        qk_scr[:, 0:QK_W] = _dot(h2, win_ref[:, o:o + QK_W]) * (DK ** -0.5)
        qk_scr[:, QK_W:2 * QK_W] = _dot(h2, win_ref[:, o + QK_W:o + 2 * QK_W])
        o += 2 * QK_W
        v_scr[...] = _dot(h2, win_ref[:, o:o + V_W]).astype(BF16)
        g = _dot(h2, win_ref[:, o + V_W:o + 2 * V_W])
        sg_scr[...] = g * _sigmoid(g)
        pre = _dot(h2, wgate_ref[...]) + gb_ref[...]
        la_scr[...] = (jnp.minimum(pre, 0.0) - jnp.log(1.0 + jnp.exp(-jnp.abs(pre)))) * (1.0 / GATE_TAU)

    def weight_copies():
        return _weight_copies(((wg_hbm.at[layer], wg_ref), (wu_hbm.at[layer], wu_ref), (wd_hbm.at[layer], wd_ref),
                               (win_hbm.at[layer, :, pl.ds(0, N_MAIN)], win_ref)), sem)

    @pl.when(i == 0)
    def _():
        copies = weight_copies()
        for c in copies:
            c.start()
        for ref in (zbuf, hcarry, qk_scr, v_scr, sg_scr, la_scr, s_scr, s_next):
            ref[...] = jnp.zeros(ref.shape, ref.dtype)
        wgate_ref[...] = jnp.dot(wgz_ref[...], w2_ref[...], preferred_element_type=F32,
                                 precision=lax.Precision.HIGHEST).astype(BF16)
        copies[0].wait()
        copies[1].wait()

    need_exact = jnp.min(la_scr[...]) < FAST_LOG_DECAY_MIN
    mixer_begin()
    gate_up_region()
    mixer_end(need_exact)

    @pl.when(i == 0)
    def _():
        copies = weight_copies()
        copies[2].wait()
        copies[3].wait()

    pl.when(i < n_tiles)(ffn_tail)


def _ffn_mixer(xa, xb, xb_tile0, cc_in, s_in, w, layer, n_prompt_tiles, n_sample_tiles, tiles_per_seq, n_prompt_seq):
    n_tiles = n_prompt_tiles + n_sample_tiles
    sample_tile = lambda t: jnp.clip(t - n_prompt_tiles, 0, n_sample_tiles - 1)
    row = lambda f: pl.BlockSpec((TM, D_MODEL), lambda i: (f(i), 0))
    per_seq = lambda r, c: pl.BlockSpec((N_CHUNK, r, c), lambda i: (sample_tile(i - 1), 0, 0))
    once = pl.Buffered(1)
    per_seq_in = lambda r, c: pl.BlockSpec((None, N_CHUNK, r, c), lambda i: (layer, sample_tile(i - 1), 0, 0),
                                           pipeline_mode=once)
    xb_spec = pl.BlockSpec((TM, D_MODEL), lambda i: (xb_tile0 + sample_tile(i), 0), pipeline_mode=once)
    n_sample_seq = n_sample_tiles * N_CHUNK
    return pl.pallas_call(
        functools.partial(_ffn_mixer_kernel, layer=layer, n_prompt_tiles=n_prompt_tiles, n_tiles=n_tiles,
                          tiles_per_seq=tiles_per_seq),
        grid=(n_tiles + 1,),
        in_specs=[row(lambda i: jnp.minimum(i, n_prompt_tiles - 1)), xb_spec]
        + [_HBM] * 4 + [_VMEM] * 10 + [per_seq_in(HIST, C_CONV), per_seq_in(QK_W, DV)],
        out_specs=[row(lambda i: jnp.minimum(i, n_tiles - 1)), row(lambda i: jnp.maximum(i - 1, 0)),
                   _VMEM, _VMEM, per_seq(HIST, C_CONV), per_seq(QK_W, DV)],
        out_shape=[jax.ShapeDtypeStruct((n_tiles * TM, D_MODEL), F32),
                   jax.ShapeDtypeStruct((n_tiles * TM, D_MODEL), BF16),
                   jax.ShapeDtypeStruct((n_prompt_seq, HIST, C_CONV), F32),
                   jax.ShapeDtypeStruct((n_prompt_seq, QK_W, DV), F32),
                   jax.ShapeDtypeStruct((n_sample_seq, HIST, C_CONV), F32),
                   jax.ShapeDtypeStruct((n_sample_seq, QK_W, DV), F32)],
        scratch_shapes=[pltpu.VMEM((D_MODEL, D_FF), BF16), pltpu.VMEM((D_MODEL, D_FF), BF16),
                        pltpu.VMEM((D_FF, D_MODEL), BF16), pltpu.VMEM((D_MODEL, N_MAIN), BF16),
                        pltpu.SemaphoreType.DMA((4,)),
                        pltpu.VMEM((TM, D_FF), BF16), pltpu.VMEM((TM, D_MODEL), BF16),
                        pltpu.VMEM((N_SLAB, N_CHUNK * SPAN, LANES), F32), pltpu.VMEM((N_SLAB, HIST, LANES), F32),
                        pltpu.VMEM((TM, 2 * QK_W), F32), pltpu.VMEM((TM, V_W), BF16),
                        pltpu.VMEM((TM, V_W), F32), pltpu.VMEM((TM, QK_W), F32),
                        pltpu.VMEM((QK_W, DV), F32), pltpu.VMEM((QK_W, DV), F32), pltpu.VMEM((CHUNK, QK_W), F32),
                        pltpu.VMEM((D_MODEL, QK_W), BF16)],
        compiler_params=pltpu.CompilerParams(dimension_semantics=("arbitrary",), vmem_limit_bytes=VMEM_LIMIT),
        name="ffn_mixer",
    )(xa, xb, w["wg1"], w["wu1"], w["wd1"], w["win"], w["n1"][layer], w["n2"][layer], w["wgz"][layer],
      w["w2"][layer], w["gb"][layer], w["dw"][layer], w["dwb"][layer], w["lng"][layer], w["lnb"][layer],
      w["gng"][layer], cc_in, s_in)


def _outproj_ffn_kernel(x1_ref, m_ref, wo_hbm, wg_hbm, wu_hbm, wd_hbm, n3_ref, nf_ref, *rest,
                        layer, final, n_prompt_tiles):
    if final:
        yp_ref, ys_ref, wo_ref, wg_ref, wu_ref, wd_ref, sem, a_ref = rest
    else:
        o_ref, wo_ref, wg_ref, wu_ref, wd_ref, sem, a_ref = rest
    i = pl.program_id(0)

    @pl.when(i == 0)
    def _():
        copies = _weight_copies(((wo_hbm.at[layer], wo_ref), (wg_hbm.at[layer], wg_ref),
                                 (wu_hbm.at[layer], wu_ref), (wd_hbm.at[layer], wd_ref)), sem)
        for c in copies:
            c.start()
        for c in copies:
            c.wait()

    x2 = x1_ref[...] + _dot(m_ref[...], wo_ref[...])
    h = _rms(x2, n3_ref[...]).astype(BF16)
    _gate_up(h, wg_ref, wu_ref, a_ref, 0, D_FF)
    x3 = x2 + 0.5 * _dot(a_ref[...], wd_ref[...])
    if final:
        y = _rms(x3, nf_ref[...])

        @pl.when(i < n_prompt_tiles)
        def _():
            yp_ref[...] = y

        @pl.when(i >= n_prompt_tiles)
        def _():
            ys_ref[...] = y
    else:
        o_ref[...] = x3


def _outproj_ffn(x1, mix, w, layer, final, n_prompt_tiles, n_sample_tiles):
    n_tiles = n_prompt_tiles + n_sample_tiles
    row = lambda f: pl.BlockSpec((TM, D_MODEL), lambda i: (f(i), 0))
    if final:
        out_specs = [row(lambda i: jnp.minimum(i, n_prompt_tiles - 1)),
                     row(lambda i: jnp.clip(i - n_prompt_tiles, 0, n_sample_tiles - 1))]
        out_shape = [jax.ShapeDtypeStruct((n_prompt_tiles * TM, D_MODEL), F32),
                     jax.ShapeDtypeStruct((n_sample_tiles * TM, D_MODEL), F32)]
    else:
        out_specs = row(lambda i: i)
        out_shape = jax.ShapeDtypeStruct((n_tiles * TM, D_MODEL), F32)
    return pl.pallas_call(
        functools.partial(_outproj_ffn_kernel, layer=layer, final=final, n_prompt_tiles=n_prompt_tiles),
        grid=(n_tiles,),
        in_specs=[row(lambda i: i), row(lambda i: i)] + [_HBM] * 4 + [_VMEM] * 2,
        out_specs=out_specs,
        out_shape=out_shape,
        scratch_shapes=[pltpu.VMEM((D_MODEL, D_MODEL), BF16), pltpu.VMEM((D_MODEL, D_FF), BF16),
                        pltpu.VMEM((D_MODEL, D_FF), BF16), pltpu.VMEM((D_FF, D_MODEL), BF16),
                        pltpu.SemaphoreType.DMA((4,)),
                        pltpu.VMEM((TM, D_FF), BF16)],
        compiler_params=pltpu.CompilerParams(dimension_semantics=("arbitrary",), vmem_limit_bytes=VMEM_LIMIT),
        name="outproj_ffn",
    )(x1, mix, w["wo"], w["wg2"], w["wu2"], w["wd2"], w["n3"][layer], w["nf"])


def kernel(x_prompt, x_sample, cache_conv, state_gla, ffn1_norm, ffn1_w_gate, ffn1_w_up, ffn1_w_down, mix_norm, w_in, conv_dw_w, conv_dw_b, conv_ln_g, conv_ln_b, gla_gate_w2, gla_gate_b, gla_norm_g, w_out, ffn2_norm, ffn2_w_gate, ffn2_w_up, ffn2_w_down, final_norm):
    bp, lp, _ = x_prompt.shape
    bs, ls, _ = x_sample.shape
    assert lp % TM == 0 and ls == CHUNK and (bs * ls) % TM == 0
    n_prompt_tiles, n_sample_tiles, tiles_per_seq = bp * lp // TM, bs * ls // TM, lp // TM

    layers = lambda f: [f(l) for l in range(DEPTH)]
    row = lambda a: layers(lambda l: a[l].reshape(1, a.shape[-1]))
    w = dict(
        wg1=ffn1_w_gate.astype(BF16), wu1=ffn1_w_up.astype(BF16), wd1=ffn1_w_down.astype(BF16),
        win=w_in.astype(BF16), wo=w_out.astype(BF16), wg2=ffn2_w_gate.astype(BF16),
        wu2=ffn2_w_up.astype(BF16), wd2=ffn2_w_down.astype(BF16), n1=row(ffn1_norm), n2=row(mix_norm), n3=row(ffn2_norm), nf=final_norm.reshape(1, D_MODEL),
        wgz=layers(lambda l: jnp.pad(w_in[l, :, N_MAIN:], ((0, 0), (0, LANES - GATE_RANK)))),
        w2=layers(lambda l: jnp.pad(gla_gate_w2[l], ((0, LANES - GATE_RANK), (0, 0)))), gb=row(gla_gate_b),
        dw=layers(lambda l: jnp.pad(conv_dw_w[l], ((0, 1), (0, 0)))), dwb=row(conv_dw_b), lng=row(conv_ln_g),
        lnb=row(conv_ln_b), gng=row(gla_norm_g),
    )
    cc_in = jnp.pad(cache_conv, ((0, 0), (0, 0), (HIST - (CONV_W - 1), 0), (0, 0)))
    s_in = state_gla.reshape(DEPTH, bs, QK_W, DV)

    xa, xb, xb_tile0 = x_prompt.reshape(bp * lp, D_MODEL), x_sample.reshape(bs * ls, D_MODEL), 0
    cc_p, s_p, cc_s, s_s = [], [], [], []
    for l in range(DEPTH):
        x1, mix, ccp, sp, ccs, ss = _ffn_mixer(xa, xb, xb_tile0, cc_in, s_in, w, l, n_prompt_tiles,
                                               n_sample_tiles, tiles_per_seq, bp)
        cc_p.append(ccp[:, HIST - (CONV_W - 1):, :])
        cc_s.append(ccs[:, HIST - (CONV_W - 1):, :])
        s_p.append(sp.reshape(bp, HEADS, DK, DV))
        s_s.append(ss.reshape(bs, HEADS, DK, DV))
        out = _outproj_ffn(x1, mix, w, l, l == DEPTH - 1, n_prompt_tiles, n_sample_tiles)
        xa, xb, xb_tile0 = out, out, n_prompt_tiles
    y_p, y_s = out
    return (y_p.reshape(bp, lp, D_MODEL), y_s.reshape(bs, ls, D_MODEL), jnp.stack(cc_p), jnp.stack(s_p),
            jnp.stack(cc_s), jnp.stack(s_s))
```

```python
import functools

import jax
import jax.numpy as jnp
from jax import lax
from jax.experimental import pallas as pl
from jax.experimental.pallas import tpu as pltpu

D_MODEL = 1024
DEPTH = 2
C_CONV = 512
CONV_W = 31
HEADS = 4
DK = 64
DV = 128
QK_W = HEADS * DK
V_W = HEADS * DV
N_MAIN = 2 * C_CONV + 2 * QK_W + 2 * V_W
GATE_RANK = 16
GATE_TAU = 16.0
D_FF = 2816
EPS = 1e-6
CHUNK = 64
SUB = 16
NSUB = CHUNK // SUB
FAST_LOG_DECAY_MIN = -4.0

LANES = 128
N_SLAB = C_CONV // LANES
HIST = 32
SPAN = HIST + CHUNK
FF_CHUNK = 256
TM = 512
TM_OUT = 1024
N_CHUNK = TM // CHUNK
VMEM_LIMIT = 58 * 1024 * 1024

F32 = jnp.float32
BF16 = jnp.bfloat16

_VMEM = pl.BlockSpec(memory_space=pltpu.VMEM)
_HBM = pl.BlockSpec(memory_space=pl.ANY)


def _sigmoid(x):
    return 1.0 / (1.0 + jnp.exp(-x))


def _rms(x, g):
    return x * lax.rsqrt(jnp.mean(x * x, axis=-1, keepdims=True) + EPS) * g


def _dot(a, b):
    return jnp.dot(a, b, preferred_element_type=F32)


def _weight_copies(pairs, sem):
    return [pltpu.make_async_copy(src, dst, sem.at[k]) for k, (src, dst) in enumerate(pairs)]


def _gate_up(h, wg_ref, wu_ref, a_ref, lo, hi, between=None):
    for c in range(lo, hi, FF_CHUNK):
        if between is not None:
            next(between, None)
        sl = slice(c, c + FF_CHUNK)
        g = _dot(h, wg_ref[:, sl])
        u = _dot(h, wu_ref[:, sl])
        a_ref[:, sl] = (g * _sigmoid(g) * u).astype(BF16)


def _gla_cumsum(la):
    r = lax.broadcasted_iota(jnp.int32, (CHUNK, CHUNK), 0)
    c = lax.broadcasted_iota(jnp.int32, (CHUNK, CHUNK), 1)
    tril = (r >= c).astype(BF16)
    la_hi = la.astype(BF16)
    r1 = la - la_hi.astype(F32)
    la_mid = r1.astype(BF16)
    la_lo = (r1 - la_mid.astype(F32)).astype(BF16)
    return _dot(tril, la_hi) + _dot(tril, la_mid) + _dot(tril, la_lo)


def _gla_operands(q, k, b):
    lane_head = lax.broadcasted_iota(jnp.int32, (CHUNK, QK_W), 1) // DK
    row_head = lax.broadcasted_iota(jnp.int32, (CHUNK, QK_W), 0) // SUB
    head_mask = lane_head == row_head
    j_idx = lax.broadcasted_iota(jnp.int32, (CHUNK, QK_W), 0)
    b_last = b[CHUNK - 1:CHUNK, :]
    qe = q * jnp.exp(b)
    qxs, kxs, qe_blocks = [], [], []
    for i in range(NSUB):
        rows = slice(i * SUB, (i + 1) * SUB)
        b_ref = b[i * SUB - 1:i * SUB, :] if i > 0 else jnp.zeros((1, QK_W), F32)
        qt = q[rows] * jnp.exp(b[rows] - b_ref)
        qxs.append(jnp.where(head_mask, jnp.concatenate([qt] * HEADS, axis=0), 0.0).astype(BF16))
        kxs.append(jnp.where(j_idx < (i + 1) * SUB, k * jnp.exp(b_ref - b), 0.0).astype(BF16))
        qe_blocks.append(jnp.where(head_mask, jnp.concatenate([qe[rows]] * HEADS, axis=0), 0.0).astype(BF16))
    qe_all = jnp.concatenate(qe_blocks, axis=0)
    kd = (k * jnp.exp(b_last - b)).astype(BF16)
    decay = jnp.exp(jnp.broadcast_to(b_last, (DV, QK_W)).T)
    return qxs, kxs, qe_all, kd, decay


def _gla_scores(qxs, kxs):
    r = lax.broadcasted_iota(jnp.int32, (CHUNK, CHUNK), 0)
    c = lax.broadcasted_iota(jnp.int32, (CHUNK, CHUNK), 1)
    row_local = r % SUB
    p_blocks = []
    for i in range(NSUB):
        a = lax.dot_general(qxs[i], kxs[i], (((1,), (1,)), ((), ())), preferred_element_type=F32)
        p_blocks.append(jnp.where(c <= i * SUB + row_local, a, 0.0).astype(BF16))
    return jnp.concatenate(p_blocks, axis=0)


def _gla_update(kd, v):
    upd = lax.dot_general(kd, v, (((0,), (0,)), ((), ())), preferred_element_type=F32)
    return jnp.concatenate([upd[h * DK:(h + 1) * DK, h * DV:(h + 1) * DV] for h in range(HEADS)], axis=0)


def _gla_output(p_all, qe_all, v, s):
    intra = _dot(p_all, v)
    inter = _dot(qe_all, s.astype(BF16))
    o_rows = []
    for i in range(NSUB):
        blocks = []
        for h in range(HEADS):
            r0 = i * CHUNK + h * SUB
            blocks.append(intra[r0:r0 + SUB, h * DV:(h + 1) * DV] + inter[r0:r0 + SUB, :])
        o_rows.append(jnp.concatenate(blocks, axis=1))
    return jnp.concatenate(o_rows, axis=0)


def _gla_tile(state_in, store_state, qk_scr, v_scr, sg_scr, la_scr, gng_ref, m_ref):
    rows = [slice(c * CHUNK, (c + 1) * CHUNK) for c in range(N_CHUNK)]
    bs = [_gla_cumsum(la_scr[r, :]) for r in rows]
    yield
    ops = [_gla_operands(qk_scr[r, 0:QK_W], qk_scr[r, QK_W:2 * QK_W], b) for r, b in zip(rows, bs)]
    ps = [_gla_scores(o[0], o[1]) for o in ops]
    upds = [_gla_update(o[3], v_scr[r, :]) for r, o in zip(rows, ops)]
    yield
    s_in, s_prev = [], None
    for c in range(N_CHUNK):
        s_in.append(state_in(c, s_prev))
        s_prev = ops[c][4] * s_in[c] + upds[c]
        store_state(c, s_prev)
    outs = [_gla_output(ps[c], ops[c][2], v_scr[rows[c], :], s_in[c]) for c in range(N_CHUNK)]
    yield
    for c in range(N_CHUNK):
        for h in range(HEADS):
            hs = slice(h * DV, (h + 1) * DV)
            oh = outs[c][:, hs]
            on = oh * lax.rsqrt(jnp.mean(oh * oh, axis=-1, keepdims=True) + EPS) * gng_ref[:, hs]
            m_ref[rows[c], C_CONV + h * DV:C_CONV + (h + 1) * DV] = (on * sg_scr[rows[c], hs]).astype(BF16)
    yield


def _gla_scores_exact(q, k_row, b, b_scr):
    b_scr[...] = b
    lane_head = lax.broadcasted_iota(jnp.int32, (CHUNK, QK_W), 1) // DK
    row_head = lax.broadcasted_iota(jnp.int32, (CHUNK, QK_W), 0) // SUB
    head_mask = lane_head == row_head
    q_rows, b_rows = [], []
    for i in range(NSUB):
        rows = slice(i * SUB, (i + 1) * SUB)
        q_rows.append(jnp.where(head_mask, jnp.concatenate([q[rows]] * HEADS, axis=0), 0.0))
        b_rows.append(jnp.concatenate([b[rows]] * HEADS, axis=0))
    q_rows = jnp.concatenate(q_rows, axis=0)
    b_rows = jnp.concatenate(b_rows, axis=0)
    n_rows = NSUB * CHUNK
    col = lax.broadcasted_iota(jnp.int32, (n_rows, LANES), 1)
    r = lax.broadcasted_iota(jnp.int32, (n_rows, LANES), 0)
    token = r // CHUNK * SUB + r % SUB

    def body(j, acc):
        p = q_rows * k_row(j) * jnp.exp(jnp.minimum(b_rows - b_scr[pl.ds(j, 1), :], 0.0))
        return jnp.where(col == j, jnp.sum(p, axis=-1, keepdims=True), acc)

    acc = lax.fori_loop(0, CHUNK, body, jnp.zeros((n_rows, LANES), F32))
    return jnp.where(col <= token, acc, 0.0)[:, 0:CHUNK].astype(BF16)


def _gla_tile_exact(state_in, store_state, qk_scr, v_scr, sg_scr, la_scr, gng_ref, m_ref, b_scr):
    s_prev = None
    for c in range(N_CHUNK):
        rows = slice(c * CHUNK, (c + 1) * CHUNK)
        q = qk_scr[rows, 0:QK_W]
        b = _gla_cumsum(la_scr[rows, :])
        _, _, qe_all, kd, decay = _gla_operands(q, qk_scr[rows, QK_W:2 * QK_W], b)
        p_all = _gla_scores_exact(q, lambda j: qk_scr[pl.ds(c * CHUNK + j, 1), QK_W:2 * QK_W], b, b_scr)
        s_in = state_in(c, s_prev)
        s_prev = decay * s_in + _gla_update(kd, v_scr[rows, :])
        store_state(c, s_prev)
        o = _gla_output(p_all, qe_all, v_scr[rows, :], s_in)
        for h in range(HEADS):
            hs = slice(h * DV, (h + 1) * DV)
            oh = o[:, hs]
            on = oh * lax.rsqrt(jnp.mean(oh * oh, axis=-1, keepdims=True) + EPS) * gng_ref[:, hs]
            m_ref[rows, C_CONV + h * DV:C_CONV + (h + 1) * DV] = (on * sg_scr[rows, hs]).astype(BF16)


def _conv_chunk(zrow, rows, zbuf, dw_ref, dwb_ref, lng_ref, lnb_ref, m_ref):
    acc = []
    for j in range(N_SLAB):
        ls = slice(j * LANES, (j + 1) * LANES)
        a = jnp.broadcast_to(dwb_ref[:, ls], (CHUNK, LANES))
        for w in range(CONV_W):
            a = a + zbuf[j, pl.ds(zrow - (CONV_W - 1) + w, CHUNK, stride=1), :] * dw_ref[w:w + 1, ls]
        acc.append(a)
    mu = sum(jnp.sum(a, axis=-1, keepdims=True) for a in acc) * (1.0 / C_CONV)
    xc = [a - mu for a in acc]
    rstd = lax.rsqrt(sum(jnp.sum(x * x, axis=-1, keepdims=True) for x in xc) * (1.0 / C_CONV) + EPS)
    for j in range(N_SLAB):
        ls = slice(j * LANES, (j + 1) * LANES)
        yn = xc[j] * rstd * lng_ref[:, ls] + lnb_ref[:, ls]
        m_ref[rows, ls] = (yn * _sigmoid(yn)).astype(BF16)


def _ffn_mixer_kernel(xa_ref, xb_ref, wg_hbm, wu_hbm, wd_hbm, win_hbm,
                      n1_ref, n2_ref, wgz_ref, w2_ref, gb_ref, dw_ref, dwb_ref, lng_ref, lnb_ref, gng_ref,
                      cci_ref, si_ref,
                      x1_ref, m_ref, ccp_ref, sp_ref, ccs_ref, ss_ref,
                      wg_ref, wu_ref, wd_ref, win_ref, sem, a_ref, h_ref, zbuf, hcarry,
                      qk_scr, v_scr, sg_scr, la_scr, s_scr, s_next, b_scr, wgate_ref,
                      *, layer, n_prompt_tiles, n_tiles, tiles_per_seq):
    i = pl.program_id(0)
    j = jnp.maximum(i - 1, 0)
    ffn_is_prompt = i < n_prompt_tiles
    mix_is_prompt = j < n_prompt_tiles
    seq_tile = j % tiles_per_seq

    def conv_work():
        for c in range(N_CHUNK):
            for sl in range(N_SLAB):
                prev = hcarry[sl] if c == 0 else zbuf[sl, (c - 1) * SPAN + CHUNK:c * SPAN, :]
                zbuf[sl, c * SPAN:c * SPAN + HIST, :] = jnp.where(
                    mix_is_prompt, prev, cci_ref[c, :, sl * LANES:(sl + 1) * LANES])
            _conv_chunk(c * SPAN + HIST, slice(c * CHUNK, (c + 1) * CHUNK), zbuf,
                        dw_ref, dwb_ref, lng_ref, lnb_ref, m_ref)
            for sl in range(N_SLAB):
                ccs_ref[c, :, sl * LANES:(sl + 1) * LANES] = zbuf[sl, c * SPAN + CHUNK:(c + 1) * SPAN, :]

    def state_in(c, s_prev):
        return jnp.where(mix_is_prompt, s_scr[...] if c == 0 else s_prev, si_ref[c])

    def store_state(c, s):
        ss_ref[c] = s
        if c == N_CHUNK - 1:
            s_next[...] = s

    def gla_work():
        return _gla_tile(state_in, store_state, qk_scr, v_scr, sg_scr, la_scr, gng_ref, m_ref)

    def x_tile():
        return jnp.where(ffn_is_prompt, xa_ref[...], xb_ref[...])

    def mixer_begin():
        @pl.when(mix_is_prompt & (seq_tile == 0))
        def _():
            hcarry[...] = jnp.zeros(hcarry.shape, F32)
            s_scr[...] = jnp.zeros(s_scr.shape, F32)

    def mixer_end(need_exact):
        @pl.when(need_exact)
        def _():
            _gla_tile_exact(state_in, store_state, qk_scr, v_scr, sg_scr, la_scr, gng_ref, m_ref, b_scr)

        s_scr[...] = s_next[...]
        hcarry[...] = zbuf[:, (N_CHUNK - 1) * SPAN + CHUNK:N_CHUNK * SPAN, :]

        @pl.when(mix_is_prompt & (seq_tile == tiles_per_seq - 1))
        def _():
            b = j // tiles_per_seq
            sp_ref[b] = s_scr[...]
            for sl in range(N_SLAB):
                ccp_ref[b, :, sl * LANES:(sl + 1) * LANES] = hcarry[sl]

    def gate_up_region():
        h_ref[...] = _rms(x_tile(), n1_ref[...]).astype(BF16)

        @pl.when(i >= 0)
        def _():
            gen = gla_work()
            _gate_up(h_ref[...], wg_ref, wu_ref, a_ref, 0, D_FF, gen)
            for _ in gen:
                pass
            conv_work()

    def ffn_tail():
        x1 = x_tile() + 0.5 * _dot(a_ref[...], wd_ref[...])
        x1_ref[...] = x1
        h2 = _rms(x1, n2_ref[...]).astype(BF16)
        ua = _dot(h2, win_ref[:, 0:C_CONV])
        ub = _dot(h2, win_ref[:, C_CONV:2 * C_CONV])
        z = ua * _sigmoid(ub)
        for c in range(N_CHUNK):
            for sl in range(N_SLAB):
                zbuf[sl, c * SPAN + HIST:(c + 1) * SPAN, :] = z[c * CHUNK:(c + 1) * CHUNK, sl * LANES:(sl + 1) * LANES]
        o = 2 * C_CONV
        qk_scr[:, 0:QK_W] = _dot(h2, win_ref[:, o:o + QK_W]) * (DK ** -0.5)
        qk_scr[:, QK_W:2 * QK_W] = _dot(h2, win_ref[:, o + QK_W:o + 2 * QK_W])
        o += 2 * QK_W
        v_scr[...] = _dot(h2, win_ref[:, o:o + V_W]).astype(BF16)
        g = _dot(h2, win_ref[:, o + V_W:o + 2 * V_W])
        sg_scr[...] = g * _sigmoid(g)
        pre = _dot(h2, wgate_ref[...]) + gb_ref[...]
        la_scr[...] = (jnp.minimum(pre, 0.0) - jnp.log(1.0 + jnp.exp(-jnp.abs(pre)))) * (1.0 / GATE_TAU)

    def weight_copies():
        return _weight_copies(((wg_hbm.at[layer], wg_ref), (wu_hbm.at[layer], wu_ref), (wd_hbm.at[layer], wd_ref),
                               (win_hbm.at[layer, :, pl.ds(0, N_MAIN)], win_ref)), sem)

    @pl.when(i == 0)
    def _():
        copies = weight_copies()
        for c in copies:
            c.start()
        for ref in (zbuf, hcarry, qk_scr, v_scr, sg_scr, la_scr, s_scr, s_next):
            ref[...] = jnp.zeros(ref.shape, ref.dtype)
        wgate_ref[...] = jnp.dot(wgz_ref[...], w2_ref[...], preferred_element_type=F32,
                                 precision=lax.Precision.HIGHEST).astype(BF16)
        copies[0].wait()
        copies[1].wait()

    need_exact = jnp.min(la_scr[...]) < FAST_LOG_DECAY_MIN
    mixer_begin()
    gate_up_region()
    mixer_end(need_exact)

    @pl.when(i == 0)
    def _():
        copies = weight_copies()
        copies[2].wait()
        copies[3].wait()

    pl.when(i < n_tiles)(ffn_tail)


def _ffn_mixer(xa, xb, xb_tile0, cc_in, s_in, w, layer, n_prompt_tiles, n_sample_tiles, tiles_per_seq, n_prompt_seq):
    n_tiles = n_prompt_tiles + n_sample_tiles
    sample_tile = lambda t: jnp.clip(t - n_prompt_tiles, 0, n_sample_tiles - 1)
    row = lambda f: pl.BlockSpec((TM, D_MODEL), lambda i: (f(i), 0))
    per_seq = lambda r, c: pl.BlockSpec((N_CHUNK, r, c), lambda i: (sample_tile(i - 1), 0, 0))
    once = pl.Buffered(1)
    per_seq_in = lambda r, c: pl.BlockSpec((None, N_CHUNK, r, c), lambda i: (layer, sample_tile(i - 1), 0, 0),
                                           pipeline_mode=once)
    xb_spec = pl.BlockSpec((TM, D_MODEL), lambda i: (xb_tile0 + sample_tile(i), 0), pipeline_mode=once)
    n_sample_seq = n_sample_tiles * N_CHUNK
    return pl.pallas_call(
        functools.partial(_ffn_mixer_kernel, layer=layer, n_prompt_tiles=n_prompt_tiles, n_tiles=n_tiles,
                          tiles_per_seq=tiles_per_seq),
        grid=(n_tiles + 1,),
        in_specs=[row(lambda i: jnp.minimum(i, n_prompt_tiles - 1)), xb_spec]
        + [_HBM] * 4 + [_VMEM] * 10 + [per_seq_in(HIST, C_CONV), per_seq_in(QK_W, DV)],
        out_specs=[row(lambda i: jnp.minimum(i, n_tiles - 1)), row(lambda i: jnp.maximum(i - 1, 0)),
                   _VMEM, _VMEM, per_seq(HIST, C_CONV), per_seq(QK_W, DV)],
        out_shape=[jax.ShapeDtypeStruct((n_tiles * TM, D_MODEL), F32),
                   jax.ShapeDtypeStruct((n_tiles * TM, D_MODEL), BF16),
                   jax.ShapeDtypeStruct((n_prompt_seq, HIST, C_CONV), F32),
                   jax.ShapeDtypeStruct((n_prompt_seq, QK_W, DV), F32),
                   jax.ShapeDtypeStruct((n_sample_seq, HIST, C_CONV), F32),
                   jax.ShapeDtypeStruct((n_sample_seq, QK_W, DV), F32)],
        scratch_shapes=[pltpu.VMEM((D_MODEL, D_FF), BF16), pltpu.VMEM((D_MODEL, D_FF), BF16),
                        pltpu.VMEM((D_FF, D_MODEL), BF16), pltpu.VMEM((D_MODEL, N_MAIN), BF16),
                        pltpu.SemaphoreType.DMA((4,)),
                        pltpu.VMEM((TM, D_FF), BF16), pltpu.VMEM((TM, D_MODEL), BF16),
                        pltpu.VMEM((N_SLAB, N_CHUNK * SPAN, LANES), F32), pltpu.VMEM((N_SLAB, HIST, LANES), F32),
                        pltpu.VMEM((TM, 2 * QK_W), F32), pltpu.VMEM((TM, V_W), BF16),
                        pltpu.VMEM((TM, V_W), F32), pltpu.VMEM((TM, QK_W), F32),
                        pltpu.VMEM((QK_W, DV), F32), pltpu.VMEM((QK_W, DV), F32), pltpu.VMEM((CHUNK, QK_W), F32),
                        pltpu.VMEM((D_MODEL, QK_W), BF16)],
        compiler_params=pltpu.CompilerParams(dimension_semantics=("arbitrary",), vmem_limit_bytes=VMEM_LIMIT),
        name="ffn_mixer",
    )(xa, xb, w["wg1"], w["wu1"], w["wd1"], w["win"], w["n1"][layer], w["n2"][layer], w["wgz"][layer],
      w["w2"][layer], w["gb"][layer], w["dw"][layer], w["dwb"][layer], w["lng"][layer], w["lnb"][layer],
      w["gng"][layer], cc_in, s_in)


def _outproj_ffn_kernel(x1_ref, m_ref, wo_hbm, wg_hbm, wu_hbm, wd_hbm, n3_ref, nf_ref, *rest,
                        layer, final, n_prompt_tiles):
    if final:
        yp_ref, ys_ref, wo_ref, wg_ref, wu_ref, wd_ref, sem, a_ref = rest
    else:
        o_ref, wo_ref, wg_ref, wu_ref, wd_ref, sem, a_ref = rest
    i = pl.program_id(0)

    @pl.when(i == 0)
    def _():
        copies = _weight_copies(((wo_hbm.at[layer], wo_ref), (wg_hbm.at[layer], wg_ref),
                                 (wu_hbm.at[layer], wu_ref), (wd_hbm.at[layer], wd_ref)), sem)
        for c in copies:
            c.start()
        for c in copies:
            c.wait()

    x2 = x1_ref[...] + _dot(m_ref[...], wo_ref[...])
    h = _rms(x2, n3_ref[...]).astype(BF16)
    _gate_up(h, wg_ref, wu_ref, a_ref, 0, D_FF)
    x3 = x2 + 0.5 * _dot(a_ref[...], wd_ref[...])
    if final:
        y = _rms(x3, nf_ref[...])

        @pl.when(i < n_prompt_tiles)
        def _():
            yp_ref[...] = y

        @pl.when(i >= n_prompt_tiles)
        def _():
            ys_ref[...] = y
    else:
        o_ref[...] = x3


def _outproj_ffn(x1, mix, w, layer, final, n_prompt_rows, n_sample_rows):
    assert n_prompt_rows % TM_OUT == 0 and n_sample_rows % TM_OUT == 0
    n_prompt_tiles, n_sample_tiles = n_prompt_rows // TM_OUT, n_sample_rows // TM_OUT
    n_tiles = n_prompt_tiles + n_sample_tiles
    row = lambda f: pl.BlockSpec((TM_OUT, D_MODEL), lambda i: (f(i), 0))
    if final:
        out_specs = [row(lambda i: jnp.minimum(i, n_prompt_tiles - 1)),
                     row(lambda i: jnp.clip(i - n_prompt_tiles, 0, n_sample_tiles - 1))]
        out_shape = [jax.ShapeDtypeStruct((n_prompt_rows, D_MODEL), F32),
                     jax.ShapeDtypeStruct((n_sample_rows, D_MODEL), F32)]
    else:
        out_specs = row(lambda i: i)
        out_shape = jax.ShapeDtypeStruct((n_prompt_rows + n_sample_rows, D_MODEL), F32)
    return pl.pallas_call(
        functools.partial(_outproj_ffn_kernel, layer=layer, final=final, n_prompt_tiles=n_prompt_tiles),
        grid=(n_tiles,),
        in_specs=[row(lambda i: i), row(lambda i: i)] + [_HBM] * 4 + [_VMEM] * 2,
        out_specs=out_specs,
        out_shape=out_shape,
        scratch_shapes=[pltpu.VMEM((D_MODEL, D_MODEL), BF16), pltpu.VMEM((D_MODEL, D_FF), BF16),
                        pltpu.VMEM((D_MODEL, D_FF), BF16), pltpu.VMEM((D_FF, D_MODEL), BF16),
                        pltpu.SemaphoreType.DMA((4,)),
                        pltpu.VMEM((TM_OUT, D_FF), BF16)],
        compiler_params=pltpu.CompilerParams(dimension_semantics=("arbitrary",), vmem_limit_bytes=VMEM_LIMIT),
        name="outproj_ffn",
    )(x1, mix, w["wo"], w["wg2"], w["wu2"], w["wd2"], w["n3"][layer], w["nf"])


def kernel(x_prompt, x_sample, cache_conv, state_gla, ffn1_norm, ffn1_w_gate, ffn1_w_up, ffn1_w_down, mix_norm, w_in, conv_dw_w, conv_dw_b, conv_ln_g, conv_ln_b, gla_gate_w2, gla_gate_b, gla_norm_g, w_out, ffn2_norm, ffn2_w_gate, ffn2_w_up, ffn2_w_down, final_norm):
    bp, lp, _ = x_prompt.shape
    bs, ls, _ = x_sample.shape
    assert lp % TM == 0 and ls == CHUNK and (bs * ls) % TM == 0
    n_prompt_tiles, n_sample_tiles, tiles_per_seq = bp * lp // TM, bs * ls // TM, lp // TM

    layers = lambda f: [f(l) for l in range(DEPTH)]
    row = lambda a: layers(lambda l: a[l].reshape(1, a.shape[-1]))
    w = dict(
        wg1=ffn1_w_gate.astype(BF16), wu1=ffn1_w_up.astype(BF16), wd1=ffn1_w_down.astype(BF16),
        win=w_in.astype(BF16), wo=w_out.astype(BF16), wg2=ffn2_w_gate.astype(BF16),
        wu2=ffn2_w_up.astype(BF16), wd2=ffn2_w_down.astype(BF16), n1=row(ffn1_norm), n2=row(mix_norm), n3=row(ffn2_norm), nf=final_norm.reshape(1, D_MODEL),
        wgz=layers(lambda l: jnp.pad(w_in[l, :, N_MAIN:], ((0, 0), (0, LANES - GATE_RANK)))),
        w2=layers(lambda l: jnp.pad(gla_gate_w2[l], ((0, LANES - GATE_RANK), (0, 0)))), gb=row(gla_gate_b),
        dw=layers(lambda l: jnp.pad(conv_dw_w[l], ((0, 1), (0, 0)))), dwb=row(conv_dw_b), lng=row(conv_ln_g),
        lnb=row(conv_ln_b), gng=row(gla_norm_g),
    )
    cc_in = jnp.pad(cache_conv, ((0, 0), (0, 0), (HIST - (CONV_W - 1), 0), (0, 0)))
    s_in = state_gla.reshape(DEPTH, bs, QK_W, DV)

    xa, xb, xb_tile0 = x_prompt.reshape(bp * lp, D_MODEL), x_sample.reshape(bs * ls, D_MODEL), 0
    cc_p, s_p, cc_s, s_s = [], [], [], []
    for l in range(DEPTH):
        x1, mix, ccp, sp, ccs, ss = _ffn_mixer(xa, xb, xb_tile0, cc_in, s_in, w, l, n_prompt_tiles,
                                               n_sample_tiles, tiles_per_seq, bp)
        cc_p.append(ccp[:, HIST - (CONV_W - 1):, :])
        cc_s.append(ccs[:, HIST - (CONV_W - 1):, :])
        s_p.append(sp.reshape(bp, HEADS, DK, DV))
        s_s.append(ss.reshape(bs, HEADS, DK, DV))
        out = _outproj_ffn(x1, mix, w, l, l == DEPTH - 1, bp * lp, bs * ls)
        xa, xb, xb_tile0 = out, out, n_prompt_tiles
    y_p, y_s = out
    return (y_p.reshape(bp, lp, D_MODEL), y_s.reshape(bs, ls, D_MODEL), jnp.stack(cc_p), jnp.stack(s_p),
            jnp.stack(cc_s), jnp.stack(s_s))
```

```python
import functools

import jax
import jax.numpy as jnp
from jax import lax
from jax.experimental import pallas as pl
from jax.experimental.pallas import tpu as pltpu

D_MODEL = 1024
DEPTH = 2
C_CONV = 512
CONV_W = 31
HEADS = 4
DK = 64
DV = 128
QK_W = HEADS * DK
V_W = HEADS * DV
N_MAIN = 2 * C_CONV + 2 * QK_W + 2 * V_W
GATE_RANK = 16
GATE_TAU = 16.0
D_FF = 2816
EPS = 1e-6
CHUNK = 64
SUB = 16
NSUB = CHUNK // SUB
FAST_LOG_DECAY_MIN = -4.0

LANES = 128
N_SLAB = C_CONV // LANES
HIST = 32
SPAN = HIST + CHUNK
FF_CHUNK = 256
TM = 512
N_CHUNK = TM // CHUNK
VMEM_LIMIT = 58 * 1024 * 1024

F32 = jnp.float32
BF16 = jnp.bfloat16

_VMEM = pl.BlockSpec(memory_space=pltpu.VMEM)
_HBM = pl.BlockSpec(memory_space=pl.ANY)


def _sigmoid(x):
    return 1.0 / (1.0 + jnp.exp(-x))


def _rms(x, g):
    return x * lax.rsqrt(jnp.mean(x * x, axis=-1, keepdims=True) + EPS) * g


def _dot(a, b):
    return jnp.dot(a, b, preferred_element_type=F32)


def _weight_copies(pairs, sem):
    return [pltpu.make_async_copy(src, dst, sem.at[k]) for k, (src, dst) in enumerate(pairs)]


def _gate_up(h, wg_ref, wu_ref, a_ref, lo, hi, between=None):
    for c in range(lo, hi, FF_CHUNK):
        if between is not None:
            next(between, None)
        sl = slice(c, c + FF_CHUNK)
        g = _dot(h, wg_ref[:, sl])
        u = _dot(h, wu_ref[:, sl])
        a_ref[:, sl] = (g * _sigmoid(g) * u).astype(BF16)


def _gla_cumsum(la):
    r = lax.broadcasted_iota(jnp.int32, (CHUNK, CHUNK), 0)
    c = lax.broadcasted_iota(jnp.int32, (CHUNK, CHUNK), 1)
    tril = (r >= c).astype(BF16)
    la_hi = la.astype(BF16)
    r1 = la - la_hi.astype(F32)
    la_mid = r1.astype(BF16)
    la_lo = (r1 - la_mid.astype(F32)).astype(BF16)
    return _dot(tril, la_hi) + _dot(tril, la_mid) + _dot(tril, la_lo)


def _gla_operands(q, k, b):
    lane_head = lax.broadcasted_iota(jnp.int32, (CHUNK, QK_W), 1) // DK
    row_head = lax.broadcasted_iota(jnp.int32, (CHUNK, QK_W), 0) // SUB
    head_mask = lane_head == row_head
    j_idx = lax.broadcasted_iota(jnp.int32, (CHUNK, QK_W), 0)
    b_last = b[CHUNK - 1:CHUNK, :]
    qe = q * jnp.exp(b)
    qxs, kxs, qe_blocks = [], [], []
    for i in range(NSUB):
        rows = slice(i * SUB, (i + 1) * SUB)
        b_ref = b[i * SUB - 1:i * SUB, :] if i > 0 else jnp.zeros((1, QK_W), F32)
        qt = q[rows] * jnp.exp(b[rows] - b_ref)
        qxs.append(jnp.where(head_mask, jnp.concatenate([qt] * HEADS, axis=0), 0.0).astype(BF16))
        kxs.append(jnp.where(j_idx < (i + 1) * SUB, k * jnp.exp(b_ref - b), 0.0).astype(BF16))
        qe_blocks.append(jnp.where(head_mask, jnp.concatenate([qe[rows]] * HEADS, axis=0), 0.0).astype(BF16))
    qe_all = jnp.concatenate(qe_blocks, axis=0)
    kd = (k * jnp.exp(b_last - b)).astype(BF16)
    decay = jnp.exp(jnp.broadcast_to(b_last, (DV, QK_W)).T)
    return qxs, kxs, qe_all, kd, decay


def _gla_scores(qxs, kxs):
    r = lax.broadcasted_iota(jnp.int32, (CHUNK, CHUNK), 0)
    c = lax.broadcasted_iota(jnp.int32, (CHUNK, CHUNK), 1)
    row_local = r % SUB
    p_blocks = []
    for i in range(NSUB):
        a = lax.dot_general(qxs[i], kxs[i], (((1,), (1,)), ((), ())), preferred_element_type=F32)
        p_blocks.append(jnp.where(c <= i * SUB + row_local, a, 0.0).astype(BF16))
    return jnp.concatenate(p_blocks, axis=0)


def _gla_update(kd, v):
    upd = lax.dot_general(kd, v, (((0,), (0,)), ((), ())), preferred_element_type=F32)
    return jnp.concatenate([upd[h * DK:(h + 1) * DK, h * DV:(h + 1) * DV] for h in range(HEADS)], axis=0)


def _gla_output(p_all, qe_all, v, s):
    intra = _dot(p_all, v)
    inter = _dot(qe_all, s.astype(BF16))
    o_rows = []
    for i in range(NSUB):
        blocks = []
        for h in range(HEADS):
            r0 = i * CHUNK + h * SUB
            blocks.append(intra[r0:r0 + SUB, h * DV:(h + 1) * DV] + inter[r0:r0 + SUB, :])
        o_rows.append(jnp.concatenate(blocks, axis=1))
    return jnp.concatenate(o_rows, axis=0)


def _gla_tile(state_in, store_state, qk_scr, v_scr, sg_scr, la_scr, gng_ref, m_ref):
    rows = [slice(c * CHUNK, (c + 1) * CHUNK) for c in range(N_CHUNK)]
    bs = [_gla_cumsum(la_scr[r, :]) for r in rows]
    yield
    ops = [_gla_operands(qk_scr[r, 0:QK_W], qk_scr[r, QK_W:2 * QK_W], b) for r, b in zip(rows, bs)]
    ps = [_gla_scores(o[0], o[1]) for o in ops]
    upds = [_gla_update(o[3], v_scr[r, :]) for r, o in zip(rows, ops)]
    yield
    s_in, s_prev = [], None
    for c in range(N_CHUNK):
        s_in.append(state_in(c, s_prev))
        s_prev = ops[c][4] * s_in[c] + upds[c]
        store_state(c, s_prev)
    outs = [_gla_output(ps[c], ops[c][2], v_scr[rows[c], :], s_in[c]) for c in range(N_CHUNK)]
    yield
    for c in range(N_CHUNK):
        for h in range(HEADS):
            hs = slice(h * DV, (h + 1) * DV)
            oh = outs[c][:, hs]
            on = oh * lax.rsqrt(jnp.mean(oh * oh, axis=-1, keepdims=True) + EPS) * gng_ref[:, hs]
            m_ref[rows[c], C_CONV + h * DV:C_CONV + (h + 1) * DV] = (on * sg_scr[rows[c], hs]).astype(BF16)
    yield


def _gla_scores_exact(q, k_row, b, b_scr):
    b_scr[...] = b
    lane_head = lax.broadcasted_iota(jnp.int32, (CHUNK, QK_W), 1) // DK
    row_head = lax.broadcasted_iota(jnp.int32, (CHUNK, QK_W), 0) // SUB
    head_mask = lane_head == row_head
    q_rows, b_rows = [], []
    for i in range(NSUB):
        rows = slice(i * SUB, (i + 1) * SUB)
        q_rows.append(jnp.where(head_mask, jnp.concatenate([q[rows]] * HEADS, axis=0), 0.0))
        b_rows.append(jnp.concatenate([b[rows]] * HEADS, axis=0))
    q_rows = jnp.concatenate(q_rows, axis=0)
    b_rows = jnp.concatenate(b_rows, axis=0)
    n_rows = NSUB * CHUNK
    col = lax.broadcasted_iota(jnp.int32, (n_rows, LANES), 1)
    r = lax.broadcasted_iota(jnp.int32, (n_rows, LANES), 0)
    token = r // CHUNK * SUB + r % SUB

    def body(j, acc):
        p = q_rows * k_row(j) * jnp.exp(jnp.minimum(b_rows - b_scr[pl.ds(j, 1), :], 0.0))
        return jnp.where(col == j, jnp.sum(p, axis=-1, keepdims=True), acc)

    acc = lax.fori_loop(0, CHUNK, body, jnp.zeros((n_rows, LANES), F32))
    return jnp.where(col <= token, acc, 0.0)[:, 0:CHUNK].astype(BF16)


def _gla_tile_exact(state_in, store_state, qk_scr, v_scr, sg_scr, la_scr, gng_ref, m_ref, b_scr):
    s_prev = None
    for c in range(N_CHUNK):
        rows = slice(c * CHUNK, (c + 1) * CHUNK)
        q = qk_scr[rows, 0:QK_W]
        b = _gla_cumsum(la_scr[rows, :])
        _, _, qe_all, kd, decay = _gla_operands(q, qk_scr[rows, QK_W:2 * QK_W], b)
        p_all = _gla_scores_exact(q, lambda j: qk_scr[pl.ds(c * CHUNK + j, 1), QK_W:2 * QK_W], b, b_scr)
        s_in = state_in(c, s_prev)
        s_prev = decay * s_in + _gla_update(kd, v_scr[rows, :])
        store_state(c, s_prev)
        o = _gla_output(p_all, qe_all, v_scr[rows, :], s_in)
        for h in range(HEADS):
            hs = slice(h * DV, (h + 1) * DV)
            oh = o[:, hs]
            on = oh * lax.rsqrt(jnp.mean(oh * oh, axis=-1, keepdims=True) + EPS) * gng_ref[:, hs]
            m_ref[rows, C_CONV + h * DV:C_CONV + (h + 1) * DV] = (on * sg_scr[rows, hs]).astype(BF16)


def _conv_chunk(zrow, rows, zbuf, dw_ref, dwb_ref, lng_ref, lnb_ref, m_ref):
    acc = []
    for j in range(N_SLAB):
        ls = slice(j * LANES, (j + 1) * LANES)
        a = jnp.broadcast_to(dwb_ref[:, ls], (CHUNK, LANES))
        for w in range(CONV_W):
            a = a + zbuf[j, pl.ds(zrow - (CONV_W - 1) + w, CHUNK, stride=1), :] * dw_ref[w:w + 1, ls]
        acc.append(a)
    mu = sum(jnp.sum(a, axis=-1, keepdims=True) for a in acc) * (1.0 / C_CONV)
    xc = [a - mu for a in acc]
    rstd = lax.rsqrt(sum(jnp.sum(x * x, axis=-1, keepdims=True) for x in xc) * (1.0 / C_CONV) + EPS)
    for j in range(N_SLAB):
        ls = slice(j * LANES, (j + 1) * LANES)
        yn = xc[j] * rstd * lng_ref[:, ls] + lnb_ref[:, ls]
        m_ref[rows, ls] = (yn * _sigmoid(yn)).astype(BF16)


def _ffn_mixer_kernel(xa_ref, xb_ref, wg_hbm, wu_hbm, wd_hbm, win_hbm,
                      n1_ref, n2_ref, wgz_ref, w2_ref, gb_ref, dw_ref, dwb_ref, lng_ref, lnb_ref, gng_ref,
                      cci_ref, si_ref,
                      x1_ref, m_ref, ccp_ref, sp_ref, ccs_ref, ss_ref,
                      wg_ref, wu_ref, wd_ref, win_ref, sem, a_ref, h_ref, zbuf, hcarry,
                      qk_scr, v_scr, sg_scr, la_scr, s_scr, s_next, b_scr, wgate_ref,
                      *, layer, n_prompt_tiles, n_tiles, tiles_per_seq):
    i = pl.program_id(0)
    j = jnp.maximum(i - 1, 0)
    ffn_is_prompt = i < n_prompt_tiles
    mix_is_prompt = j < n_prompt_tiles
    seq_tile = j % tiles_per_seq

    def conv_work():
        for c in range(N_CHUNK):
            for sl in range(N_SLAB):
                lo = HIST - (CONV_W - 1)
                prev = hcarry[sl, lo:, :] if c == 0 else zbuf[sl, (c - 1) * SPAN + CHUNK + lo:c * SPAN, :]
                zbuf[sl, c * SPAN + lo:c * SPAN + HIST, :] = jnp.where(
                    mix_is_prompt, prev, cci_ref[c, :, sl * LANES:(sl + 1) * LANES])
            _conv_chunk(c * SPAN + HIST, slice(c * CHUNK, (c + 1) * CHUNK), zbuf,
                        dw_ref, dwb_ref, lng_ref, lnb_ref, m_ref)
            for sl in range(N_SLAB):
                ccs_ref[c, :, sl * LANES:(sl + 1) * LANES] = zbuf[sl, c * SPAN + CHUNK:(c + 1) * SPAN, :]

    def state_in(c, s_prev):
        return jnp.where(mix_is_prompt, s_scr[...] if c == 0 else s_prev, si_ref[c])

    def store_state(c, s):
        ss_ref[c] = s
        if c == N_CHUNK - 1:
            s_next[...] = s

    def gla_work():
        return _gla_tile(state_in, store_state, qk_scr, v_scr, sg_scr, la_scr, gng_ref, m_ref)

    def x_tile():
        return jnp.where(ffn_is_prompt, xa_ref[...], xb_ref[...])

    def mixer_begin():
        @pl.when(mix_is_prompt & (seq_tile == 0))
        def _():
            hcarry[...] = jnp.zeros(hcarry.shape, F32)
            s_scr[...] = jnp.zeros(s_scr.shape, F32)

    def mixer_end(need_exact):
        @pl.when(need_exact)
        def _():
            _gla_tile_exact(state_in, store_state, qk_scr, v_scr, sg_scr, la_scr, gng_ref, m_ref, b_scr)

        s_scr[...] = s_next[...]
        hcarry[...] = zbuf[:, (N_CHUNK - 1) * SPAN + CHUNK:N_CHUNK * SPAN, :]

        @pl.when(mix_is_prompt & (seq_tile == tiles_per_seq - 1))
        def _():
            b = j // tiles_per_seq
            sp_ref[b] = s_scr[...]
            for sl in range(N_SLAB):
                ccp_ref[b, :, sl * LANES:(sl + 1) * LANES] = hcarry[sl]

    def gate_up_region():
        h_ref[...] = _rms(x_tile(), n1_ref[...]).astype(BF16)

        @pl.when(i >= 0)
        def _():
            gen = gla_work()
            _gate_up(h_ref[...], wg_ref, wu_ref, a_ref, 0, D_FF, gen)
            for _ in gen:
                pass
            conv_work()

    def ffn_tail():
        x1 = x_tile() + 0.5 * _dot(a_ref[...], wd_ref[...])
        x1_ref[...] = x1
        h2 = _rms(x1, n2_ref[...]).astype(BF16)
        ua = _dot(h2, win_ref[:, 0:C_CONV])
        ub = _dot(h2, win_ref[:, C_CONV:2 * C_CONV])
        z = ua * _sigmoid(ub)
        for c in range(N_CHUNK):
            for sl in range(N_SLAB):
                zbuf[sl, c * SPAN + HIST:(c + 1) * SPAN, :] = z[c * CHUNK:(c + 1) * CHUNK, sl * LANES:(sl + 1) * LANES]
        o = 2 * C_CONV
        qk_scr[:, 0:QK_W] = _dot(h2, win_ref[:, o:o + QK_W]) * (DK ** -0.5)
        qk_scr[:, QK_W:2 * QK_W] = _dot(h2, win_ref[:, o + QK_W:o + 2 * QK_W])
        o += 2 * QK_W
        v_scr[...] = _dot(h2, win_ref[:, o:o + V_W]).astype(BF16)
        g = _dot(h2, win_ref[:, o + V_W:o + 2 * V_W])
        sg_scr[...] = g * _sigmoid(g)
        pre = _dot(h2, wgate_ref[...]) + gb_ref[...]
        la_scr[...] = (jnp.minimum(pre, 0.0) - jnp.log(1.0 + jnp.exp(-jnp.abs(pre)))) * (1.0 / GATE_TAU)

    def weight_copies():
        return _weight_copies(((wg_hbm.at[layer], wg_ref), (wu_hbm.at[layer], wu_ref), (wd_hbm.at[layer], wd_ref),
                               (win_hbm.at[layer, :, pl.ds(0, N_MAIN)], win_ref)), sem)

    @pl.when(i == 0)
    def _():
        copies = weight_copies()
        for c in copies:
            c.start()
        for ref in (zbuf, hcarry, qk_scr, v_scr, sg_scr, la_scr, s_scr, s_next):
            ref[...] = jnp.zeros(ref.shape, ref.dtype)
        wgate_ref[...] = jnp.dot(wgz_ref[...], w2_ref[...], preferred_element_type=F32,
                                 precision=lax.Precision.HIGHEST).astype(BF16)
        copies[0].wait()
        copies[1].wait()

    need_exact = jnp.min(la_scr[...]) < FAST_LOG_DECAY_MIN
    mixer_begin()
    gate_up_region()
    mixer_end(need_exact)

    @pl.when(i == 0)
    def _():
        copies = weight_copies()
        copies[2].wait()
        copies[3].wait()

    pl.when(i < n_tiles)(ffn_tail)


def _ffn_mixer(xa, xb, xb_tile0, cc_in, s_in, w, layer, n_prompt_tiles, n_sample_tiles, tiles_per_seq, n_prompt_seq):
    n_tiles = n_prompt_tiles + n_sample_tiles
    sample_tile = lambda t: jnp.clip(t - n_prompt_tiles, 0, n_sample_tiles - 1)
    row = lambda f: pl.BlockSpec((TM, D_MODEL), lambda i: (f(i), 0))
    per_seq = lambda r, c: pl.BlockSpec((N_CHUNK, r, c), lambda i: (sample_tile(i - 1), 0, 0))
    once = pl.Buffered(1)
    per_seq_in = lambda r, c: pl.BlockSpec((None, N_CHUNK, r, c), lambda i: (layer, sample_tile(i - 1), 0, 0),
                                           pipeline_mode=once)
    xb_spec = pl.BlockSpec((TM, D_MODEL), lambda i: (xb_tile0 + sample_tile(i), 0), pipeline_mode=once)
    n_sample_seq = n_sample_tiles * N_CHUNK
    return pl.pallas_call(
        functools.partial(_ffn_mixer_kernel, layer=layer, n_prompt_tiles=n_prompt_tiles, n_tiles=n_tiles,
                          tiles_per_seq=tiles_per_seq),
        grid=(n_tiles + 1,),
        in_specs=[row(lambda i: jnp.minimum(i, n_prompt_tiles - 1)), xb_spec]
        + [_HBM] * 4 + [_VMEM] * 10 + [per_seq_in(CONV_W - 1, C_CONV), per_seq_in(QK_W, DV)],
        out_specs=[row(lambda i: jnp.minimum(i, n_tiles - 1)), row(lambda i: jnp.maximum(i - 1, 0)),
                   _VMEM, _VMEM, per_seq(HIST, C_CONV), per_seq(QK_W, DV)],
        out_shape=[jax.ShapeDtypeStruct((n_tiles * TM, D_MODEL), F32),
                   jax.ShapeDtypeStruct((n_tiles * TM, D_MODEL), BF16),
                   jax.ShapeDtypeStruct((n_prompt_seq, HIST, C_CONV), F32),
                   jax.ShapeDtypeStruct((n_prompt_seq, QK_W, DV), F32),
                   jax.ShapeDtypeStruct((n_sample_seq, HIST, C_CONV), F32),
                   jax.ShapeDtypeStruct((n_sample_seq, QK_W, DV), F32)],
        scratch_shapes=[pltpu.VMEM((D_MODEL, D_FF), BF16), pltpu.VMEM((D_MODEL, D_FF), BF16),
                        pltpu.VMEM((D_FF, D_MODEL), BF16), pltpu.VMEM((D_MODEL, N_MAIN), BF16),
                        pltpu.SemaphoreType.DMA((4,)),
                        pltpu.VMEM((TM, D_FF), BF16), pltpu.VMEM((TM, D_MODEL), BF16),
                        pltpu.VMEM((N_SLAB, N_CHUNK * SPAN, LANES), F32), pltpu.VMEM((N_SLAB, HIST, LANES), F32),
                        pltpu.VMEM((TM, 2 * QK_W), F32), pltpu.VMEM((TM, V_W), BF16),
                        pltpu.VMEM((TM, V_W), F32), pltpu.VMEM((TM, QK_W), F32),
                        pltpu.VMEM((QK_W, DV), F32), pltpu.VMEM((QK_W, DV), F32), pltpu.VMEM((CHUNK, QK_W), F32),
                        pltpu.VMEM((D_MODEL, QK_W), BF16)],
        compiler_params=pltpu.CompilerParams(dimension_semantics=("arbitrary",), vmem_limit_bytes=VMEM_LIMIT),
        name="ffn_mixer",
    )(xa, xb, w["wg1"], w["wu1"], w["wd1"], w["win"], w["n1"][layer], w["n2"][layer], w["wgz"][layer],
      w["w2"][layer], w["gb"][layer], w["dw"][layer], w["dwb"][layer], w["lng"][layer], w["lnb"][layer],
      w["gng"][layer], cc_in, s_in)


def _outproj_ffn_kernel(x1_ref, m_ref, wo_hbm, wg_hbm, wu_hbm, wd_hbm, n3_ref, nf_ref, *rest,
                        layer, final, n_prompt_tiles):
    if final:
        yp_ref, ys_ref, wo_ref, wg_ref, wu_ref, wd_ref, sem, a_ref = rest
    else:
        o_ref, wo_ref, wg_ref, wu_ref, wd_ref, sem, a_ref = rest
    i = pl.program_id(0)

    @pl.when(i == 0)
    def _():
        copies = _weight_copies(((wo_hbm.at[layer], wo_ref), (wg_hbm.at[layer], wg_ref),
                                 (wu_hbm.at[layer], wu_ref), (wd_hbm.at[layer], wd_ref)), sem)
        for c in copies:
            c.start()
        for c in copies:
            c.wait()

    x2 = x1_ref[...] + _dot(m_ref[...], wo_ref[...])
    h = _rms(x2, n3_ref[...]).astype(BF16)
    _gate_up(h, wg_ref, wu_ref, a_ref, 0, D_FF)
    x3 = x2 + 0.5 * _dot(a_ref[...], wd_ref[...])
    if final:
        y = _rms(x3, nf_ref[...])

        @pl.when(i < n_prompt_tiles)
        def _():
            yp_ref[...] = y

        @pl.when(i >= n_prompt_tiles)
        def _():
            ys_ref[...] = y
    else:
        o_ref[...] = x3


def _outproj_ffn(x1, mix, w, layer, final, n_prompt_tiles, n_sample_tiles):
    n_tiles = n_prompt_tiles + n_sample_tiles
    row = lambda f: pl.BlockSpec((TM, D_MODEL), lambda i: (f(i), 0))
    if final:
        out_specs = [row(lambda i: jnp.minimum(i, n_prompt_tiles - 1)),
                     row(lambda i: jnp.clip(i - n_prompt_tiles, 0, n_sample_tiles - 1))]
        out_shape = [jax.ShapeDtypeStruct((n_prompt_tiles * TM, D_MODEL), F32),
                     jax.ShapeDtypeStruct((n_sample_tiles * TM, D_MODEL), F32)]
    else:
        out_specs = row(lambda i: i)
        out_shape = jax.ShapeDtypeStruct((n_tiles * TM, D_MODEL), F32)
    return pl.pallas_call(
        functools.partial(_outproj_ffn_kernel, layer=layer, final=final, n_prompt_tiles=n_prompt_tiles),
        grid=(n_tiles,),
        in_specs=[row(lambda i: i), row(lambda i: i)] + [_HBM] * 4 + [_VMEM] * 2,
        out_specs=out_specs,
        out_shape=out_shape,
        scratch_shapes=[pltpu.VMEM((D_MODEL, D_MODEL), BF16), pltpu.VMEM((D_MODEL, D_FF), BF16),
                        pltpu.VMEM((D_MODEL, D_FF), BF16), pltpu.VMEM((D_FF, D_MODEL), BF16),
                        pltpu.SemaphoreType.DMA((4,)),
                        pltpu.VMEM((TM, D_FF), BF16)],
        compiler_params=pltpu.CompilerParams(dimension_semantics=("arbitrary",), vmem_limit_bytes=VMEM_LIMIT),
        name="outproj_ffn",
    )(x1, mix, w["wo"], w["wg2"], w["wu2"], w["wd2"], w["n3"][layer], w["nf"])


def kernel(x_prompt, x_sample, cache_conv, state_gla, ffn1_norm, ffn1_w_gate, ffn1_w_up, ffn1_w_down, mix_norm, w_in, conv_dw_w, conv_dw_b, conv_ln_g, conv_ln_b, gla_gate_w2, gla_gate_b, gla_norm_g, w_out, ffn2_norm, ffn2_w_gate, ffn2_w_up, ffn2_w_down, final_norm):
    bp, lp, _ = x_prompt.shape
    bs, ls, _ = x_sample.shape
    assert lp % TM == 0 and ls == CHUNK and (bs * ls) % TM == 0
    n_prompt_tiles, n_sample_tiles, tiles_per_seq = bp * lp // TM, bs * ls // TM, lp // TM

    layers = lambda f: [f(l) for l in range(DEPTH)]
    row = lambda a: layers(lambda l: a[l].reshape(1, a.shape[-1]))
    w = dict(
        wg1=ffn1_w_gate.astype(BF16), wu1=ffn1_w_up.astype(BF16), wd1=ffn1_w_down.astype(BF16),
        win=w_in.astype(BF16), wo=w_out.astype(BF16), wg2=ffn2_w_gate.astype(BF16),
        wu2=ffn2_w_up.astype(BF16), wd2=ffn2_w_down.astype(BF16), n1=row(ffn1_norm), n2=row(mix_norm), n3=row(ffn2_norm), nf=final_norm.reshape(1, D_MODEL),
        wgz=layers(lambda l: jnp.pad(w_in[l, :, N_MAIN:], ((0, 0), (0, LANES - GATE_RANK)))),
        w2=layers(lambda l: jnp.pad(gla_gate_w2[l], ((0, LANES - GATE_RANK), (0, 0)))), gb=row(gla_gate_b),
        dw=layers(lambda l: jnp.pad(conv_dw_w[l], ((0, 1), (0, 0)))), dwb=row(conv_dw_b), lng=row(conv_ln_g),
        lnb=row(conv_ln_b), gng=row(gla_norm_g),
    )
    cc_in = cache_conv
    s_in = state_gla.reshape(DEPTH, bs, QK_W, DV)

    xa, xb, xb_tile0 = x_prompt.reshape(bp * lp, D_MODEL), x_sample.reshape(bs * ls, D_MODEL), 0
    cc_p, s_p, cc_s, s_s = [], [], [], []
    for l in range(DEPTH):
        x1, mix, ccp, sp, ccs, ss = _ffn_mixer(xa, xb, xb_tile0, cc_in, s_in, w, l, n_prompt_tiles,
                                               n_sample_tiles, tiles_per_seq, bp)
        cc_p.append(ccp[:, HIST - (CONV_W - 1):, :])
        cc_s.append(ccs[:, HIST - (CONV_W - 1):, :])
        s_p.append(sp.reshape(bp, HEADS, DK, DV))
        s_s.append(ss.reshape(bs, HEADS, DK, DV))
        out = _outproj_ffn(x1, mix, w, l, l == DEPTH - 1, n_prompt_tiles, n_sample_tiles)
        xa, xb, xb_tile0 = out, out, n_prompt_tiles
    y_p, y_s = out
    return (y_p.reshape(bp, lp, D_MODEL), y_s.reshape(bs, ls, D_MODEL), jnp.stack(cc_p), jnp.stack(s_p),
            jnp.stack(cc_s), jnp.stack(s_s))
```

```python
import functools

import jax
import jax.numpy as jnp
from jax import lax
from jax.experimental import pallas as pl
from jax.experimental.pallas import tpu as pltpu

D_MODEL = 1024
DEPTH = 2
C_CONV = 512
CONV_W = 31
HEADS = 4
DK = 64
DV = 128
QK_W = HEADS * DK
V_W = HEADS * DV
N_MAIN = 2 * C_CONV + 2 * QK_W + 2 * V_W
GATE_RANK = 16
GATE_TAU = 16.0
D_FF = 2816
EPS = 1e-6
CHUNK = 64
SUB = 16
NSUB = CHUNK // SUB
FAST_LOG_DECAY_MIN = -4.0

LANES = 128
N_SLAB = C_CONV // LANES
HIST = 32
SPAN = HIST + CHUNK
FF_CHUNK = 256
TM = 512
N_CHUNK = TM // CHUNK
VMEM_LIMIT = 58 * 1024 * 1024

F32 = jnp.float32
BF16 = jnp.bfloat16

_VMEM = pl.BlockSpec(memory_space=pltpu.VMEM)
_HBM = pl.BlockSpec(memory_space=pl.ANY)


def _sigmoid(x):
    return 1.0 / (1.0 + jnp.exp(-x))


def _rms(x, g):
    return x * lax.rsqrt(jnp.mean(x * x, axis=-1, keepdims=True) + EPS) * g


def _dot(a, b):
    return jnp.dot(a, b, preferred_element_type=F32)


def _weight_copies(pairs, sem):
    return [pltpu.make_async_copy(src, dst, sem.at[k]) for k, (src, dst) in enumerate(pairs)]


def _gate_up(h, wg_ref, wu_ref, a_ref, lo, hi, between=None):
    for c in range(lo, hi, FF_CHUNK):
        if between is not None:
            next(between, None)
        sl = slice(c, c + FF_CHUNK)
        g = _dot(h, wg_ref[:, sl])
        u = _dot(h, wu_ref[:, sl])
        a_ref[:, sl] = (g * _sigmoid(g) * u).astype(BF16)


def _gla_cumsum(la):
    r = lax.broadcasted_iota(jnp.int32, (CHUNK, CHUNK), 0)
    c = lax.broadcasted_iota(jnp.int32, (CHUNK, CHUNK), 1)
    tril = (r >= c).astype(BF16)
    la_hi = la.astype(BF16)
    r1 = la - la_hi.astype(F32)
    la_mid = r1.astype(BF16)
    la_lo = (r1 - la_mid.astype(F32)).astype(BF16)
    return _dot(tril, la_hi) + _dot(tril, la_mid) + _dot(tril, la_lo)


def _gla_operands(q, k, b):
    lane_head = lax.broadcasted_iota(jnp.int32, (CHUNK, QK_W), 1) // DK
    row_head = lax.broadcasted_iota(jnp.int32, (CHUNK, QK_W), 0) // SUB
    head_mask = lane_head == row_head
    j_idx = lax.broadcasted_iota(jnp.int32, (CHUNK, QK_W), 0)
    b_last = b[CHUNK - 1:CHUNK, :]
    qe = q * jnp.exp(b)
    qxs, kxs, qe_blocks = [], [], []
    for i in range(NSUB):
        rows = slice(i * SUB, (i + 1) * SUB)
        b_ref = b[i * SUB - 1:i * SUB, :] if i > 0 else jnp.zeros((1, QK_W), F32)
        qt = q[rows] * jnp.exp(b[rows] - b_ref)
        qxs.append(jnp.where(head_mask, jnp.concatenate([qt] * HEADS, axis=0), 0.0).astype(BF16))
        kxs.append(jnp.where(j_idx < (i + 1) * SUB, k * jnp.exp(b_ref - b), 0.0).astype(BF16))
        qe_blocks.append(jnp.where(head_mask, jnp.concatenate([qe[rows]] * HEADS, axis=0), 0.0).astype(BF16))
    qe_all = jnp.concatenate(qe_blocks, axis=0)
    kd = (k * jnp.exp(b_last - b)).astype(BF16)
    decay = jnp.exp(jnp.broadcast_to(b_last, (DV, QK_W)).T)
    return qxs, kxs, qe_all, kd, decay


def _gla_scores(qxs, kxs):
    r = lax.broadcasted_iota(jnp.int32, (CHUNK, CHUNK), 0)
    c = lax.broadcasted_iota(jnp.int32, (CHUNK, CHUNK), 1)
    row_local = r % SUB
    p_blocks = []
    for i in range(NSUB):
        a = lax.dot_general(qxs[i], kxs[i], (((1,), (1,)), ((), ())), preferred_element_type=F32)
        p_blocks.append(jnp.where(c <= i * SUB + row_local, a, 0.0).astype(BF16))
    return jnp.concatenate(p_blocks, axis=0)


def _gla_update(kd, v):
    kdt = kd.T
    return jnp.concatenate([_dot(kdt[h * DK:(h + 1) * DK, :], v[:, h * DV:(h + 1) * DV]) for h in range(HEADS)],
                           axis=0)


def _gla_output(p_all, qe_all, v, s):
    inter = _dot(qe_all, s.astype(BF16))
    heads = []
    for h in range(HEADS):
        head_rows = [slice(i * CHUNK + h * SUB, i * CHUNK + (h + 1) * SUB) for i in range(NSUB)]
        ph = jnp.concatenate([p_all[r] for r in head_rows], axis=0)
        ih = jnp.concatenate([inter[r] for r in head_rows], axis=0)
        heads.append(_dot(ph, v[:, h * DV:(h + 1) * DV]) + ih)
    return jnp.concatenate(heads, axis=1)


def _gla_tile(state_in, store_state, qk_scr, v_scr, sg_scr, la_scr, gng_ref, m_ref):
    rows = [slice(c * CHUNK, (c + 1) * CHUNK) for c in range(N_CHUNK)]
    bs = [_gla_cumsum(la_scr[r, :]) for r in rows]
    yield
    ops = [_gla_operands(qk_scr[r, 0:QK_W], qk_scr[r, QK_W:2 * QK_W], b) for r, b in zip(rows, bs)]
    ps = [_gla_scores(o[0], o[1]) for o in ops]
    upds = [_gla_update(o[3], v_scr[r, :]) for r, o in zip(rows, ops)]
    yield
    s_in, s_prev = [], None
    for c in range(N_CHUNK):
        s_in.append(state_in(c, s_prev))
        s_prev = ops[c][4] * s_in[c] + upds[c]
        store_state(c, s_prev)
    outs = [_gla_output(ps[c], ops[c][2], v_scr[rows[c], :], s_in[c]) for c in range(N_CHUNK)]
    yield
    for c in range(N_CHUNK):
        for h in range(HEADS):
            hs = slice(h * DV, (h + 1) * DV)
            oh = outs[c][:, hs]
            on = oh * lax.rsqrt(jnp.mean(oh * oh, axis=-1, keepdims=True) + EPS) * gng_ref[:, hs]
            m_ref[rows[c], C_CONV + h * DV:C_CONV + (h + 1) * DV] = (on * sg_scr[rows[c], hs]).astype(BF16)
    yield


def _gla_scores_exact(q, k_row, b, b_scr):
    b_scr[...] = b
    lane_head = lax.broadcasted_iota(jnp.int32, (CHUNK, QK_W), 1) // DK
    row_head = lax.broadcasted_iota(jnp.int32, (CHUNK, QK_W), 0) // SUB
    head_mask = lane_head == row_head
    q_rows, b_rows = [], []
    for i in range(NSUB):
        rows = slice(i * SUB, (i + 1) * SUB)
        q_rows.append(jnp.where(head_mask, jnp.concatenate([q[rows]] * HEADS, axis=0), 0.0))
        b_rows.append(jnp.concatenate([b[rows]] * HEADS, axis=0))
    q_rows = jnp.concatenate(q_rows, axis=0)
    b_rows = jnp.concatenate(b_rows, axis=0)
    n_rows = NSUB * CHUNK
    col = lax.broadcasted_iota(jnp.int32, (n_rows, LANES), 1)
    r = lax.broadcasted_iota(jnp.int32, (n_rows, LANES), 0)
    token = r // CHUNK * SUB + r % SUB

    def body(j, acc):
        p = q_rows * k_row(j) * jnp.exp(jnp.minimum(b_rows - b_scr[pl.ds(j, 1), :], 0.0))
        return jnp.where(col == j, jnp.sum(p, axis=-1, keepdims=True), acc)

    acc = lax.fori_loop(0, CHUNK, body, jnp.zeros((n_rows, LANES), F32))
    return jnp.where(col <= token, acc, 0.0)[:, 0:CHUNK].astype(BF16)


def _gla_tile_exact(state_in, store_state, qk_scr, v_scr, sg_scr, la_scr, gng_ref, m_ref, b_scr):
    s_prev = None
    for c in range(N_CHUNK):
        rows = slice(c * CHUNK, (c + 1) * CHUNK)
        q = qk_scr[rows, 0:QK_W]
        b = _gla_cumsum(la_scr[rows, :])
        _, _, qe_all, kd, decay = _gla_operands(q, qk_scr[rows, QK_W:2 * QK_W], b)
        p_all = _gla_scores_exact(q, lambda j: qk_scr[pl.ds(c * CHUNK + j, 1), QK_W:2 * QK_W], b, b_scr)
        s_in = state_in(c, s_prev)
        s_prev = decay * s_in + _gla_update(kd, v_scr[rows, :])
        store_state(c, s_prev)
        o = _gla_output(p_all, qe_all, v_scr[rows, :], s_in)
        for h in range(HEADS):
            hs = slice(h * DV, (h + 1) * DV)
            oh = o[:, hs]
            on = oh * lax.rsqrt(jnp.mean(oh * oh, axis=-1, keepdims=True) + EPS) * gng_ref[:, hs]
            m_ref[rows, C_CONV + h * DV:C_CONV + (h + 1) * DV] = (on * sg_scr[rows, hs]).astype(BF16)


def _conv_chunk(zrow, rows, zbuf, dw_ref, dwb_ref, lng_ref, lnb_ref, m_ref):
    acc = []
    for j in range(N_SLAB):
        ls = slice(j * LANES, (j + 1) * LANES)
        a = jnp.broadcast_to(dwb_ref[:, ls], (CHUNK, LANES))
        for w in range(CONV_W):
            a = a + zbuf[j, pl.ds(zrow - (CONV_W - 1) + w, CHUNK, stride=1), :] * dw_ref[w:w + 1, ls]
        acc.append(a)
    mu = sum(jnp.sum(a, axis=-1, keepdims=True) for a in acc) * (1.0 / C_CONV)
    xc = [a - mu for a in acc]
    rstd = lax.rsqrt(sum(jnp.sum(x * x, axis=-1, keepdims=True) for x in xc) * (1.0 / C_CONV) + EPS)
    for j in range(N_SLAB):
        ls = slice(j * LANES, (j + 1) * LANES)
        yn = xc[j] * rstd * lng_ref[:, ls] + lnb_ref[:, ls]
        m_ref[rows, ls] = (yn * _sigmoid(yn)).astype(BF16)


def _ffn_mixer_kernel(xa_ref, xb_ref, wg_hbm, wu_hbm, wd_hbm, win_hbm,
                      n1_ref, n2_ref, wgz_ref, w2_ref, gb_ref, dw_ref, dwb_ref, lng_ref, lnb_ref, gng_ref,
                      cci_ref, si_ref,
                      x1_ref, m_ref, ccp_ref, sp_ref, ccs_ref, ss_ref,
                      wg_ref, wu_ref, wd_ref, win_ref, sem, a_ref, h_ref, zbuf, hcarry,
                      qk_scr, v_scr, sg_scr, la_scr, s_scr, s_next, b_scr, wgate_ref,
                      *, layer, n_prompt_tiles, n_tiles, tiles_per_seq):
    i = pl.program_id(0)
    j = jnp.maximum(i - 1, 0)
    ffn_is_prompt = i < n_prompt_tiles
    mix_is_prompt = j < n_prompt_tiles
    seq_tile = j % tiles_per_seq

    def conv_work():
        for c in range(N_CHUNK):
            for sl in range(N_SLAB):
                prev = hcarry[sl] if c == 0 else zbuf[sl, (c - 1) * SPAN + CHUNK:c * SPAN, :]
                zbuf[sl, c * SPAN:c * SPAN + HIST, :] = jnp.where(
                    mix_is_prompt, prev, cci_ref[c, :, sl * LANES:(sl + 1) * LANES])
            _conv_chunk(c * SPAN + HIST, slice(c * CHUNK, (c + 1) * CHUNK), zbuf,
                        dw_ref, dwb_ref, lng_ref, lnb_ref, m_ref)
            for sl in range(N_SLAB):
                ccs_ref[c, :, sl * LANES:(sl + 1) * LANES] = zbuf[sl, c * SPAN + CHUNK:(c + 1) * SPAN, :]

    def state_in(c, s_prev):
        return jnp.where(mix_is_prompt, s_scr[...] if c == 0 else s_prev, si_ref[c])

    def store_state(c, s):
        ss_ref[c] = s
        if c == N_CHUNK - 1:
            s_next[...] = s

    def gla_work():
        return _gla_tile(state_in, store_state, qk_scr, v_scr, sg_scr, la_scr, gng_ref, m_ref)

    def x_tile():
        return jnp.where(ffn_is_prompt, xa_ref[...], xb_ref[...])

    def mixer_begin():
        @pl.when(mix_is_prompt & (seq_tile == 0))
        def _():
            hcarry[...] = jnp.zeros(hcarry.shape, F32)
            s_scr[...] = jnp.zeros(s_scr.shape, F32)

    def mixer_end(need_exact):
        @pl.when(need_exact)
        def _():
            _gla_tile_exact(state_in, store_state, qk_scr, v_scr, sg_scr, la_scr, gng_ref, m_ref, b_scr)

        s_scr[...] = s_next[...]
        hcarry[...] = zbuf[:, (N_CHUNK - 1) * SPAN + CHUNK:N_CHUNK * SPAN, :]

        @pl.when(mix_is_prompt & (seq_tile == tiles_per_seq - 1))
        def _():
            b = j // tiles_per_seq
            sp_ref[b] = s_scr[...]
            for sl in range(N_SLAB):
                ccp_ref[b, :, sl * LANES:(sl + 1) * LANES] = hcarry[sl]

    def gate_up_region():
        h_ref[...] = _rms(x_tile(), n1_ref[...]).astype(BF16)

        @pl.when(i >= 0)
        def _():
            gen = gla_work()
            _gate_up(h_ref[...], wg_ref, wu_ref, a_ref, 0, D_FF, gen)
            for _ in gen:
                pass
            conv_work()

    def ffn_tail():
        x1 = x_tile() + 0.5 * _dot(a_ref[...], wd_ref[...])
        x1_ref[...] = x1
        h2 = _rms(x1, n2_ref[...]).astype(BF16)
        ua = _dot(h2, win_ref[:, 0:C_CONV])
        ub = _dot(h2, win_ref[:, C_CONV:2 * C_CONV])
        z = ua * _sigmoid(ub)
        for c in range(N_CHUNK):
            for sl in range(N_SLAB):
                zbuf[sl, c * SPAN + HIST:(c + 1) * SPAN, :] = z[c * CHUNK:(c + 1) * CHUNK, sl * LANES:(sl + 1) * LANES]
        o = 2 * C_CONV
        qk_scr[:, 0:QK_W] = _dot(h2, win_ref[:, o:o + QK_W]) * (DK ** -0.5)
        qk_scr[:, QK_W:2 * QK_W] = _dot(h2, win_ref[:, o + QK_W:o + 2 * QK_W])
        o += 2 * QK_W
        v_scr[...] = _dot(h2, win_ref[:, o:o + V_W]).astype(BF16)
        g = _dot(h2, win_ref[:, o + V_W:o + 2 * V_W])
        sg_scr[...] = g * _sigmoid(g)
        pre = _dot(h2, wgate_ref[...]) + gb_ref[...]
        la_scr[...] = (jnp.minimum(pre, 0.0) - jnp.log(1.0 + jnp.exp(-jnp.abs(pre)))) * (1.0 / GATE_TAU)

    def weight_copies():
        return _weight_copies(((wg_hbm.at[layer], wg_ref), (wu_hbm.at[layer], wu_ref), (wd_hbm.at[layer], wd_ref),
                               (win_hbm.at[layer, :, pl.ds(0, N_MAIN)], win_ref)), sem)

    @pl.when(i == 0)
    def _():
        copies = weight_copies()
        for c in copies:
            c.start()
        for ref in (zbuf, hcarry, qk_scr, v_scr, sg_scr, la_scr, s_scr, s_next):
            ref[...] = jnp.zeros(ref.shape, ref.dtype)
        wgate_ref[...] = jnp.dot(wgz_ref[...], w2_ref[...], preferred_element_type=F32,
                                 precision=lax.Precision.HIGHEST).astype(BF16)
        copies[0].wait()
        copies[1].wait()

    need_exact = jnp.min(la_scr[...]) < FAST_LOG_DECAY_MIN
    mixer_begin()
    gate_up_region()
    mixer_end(need_exact)

    @pl.when(i == 0)
    def _():
        copies = weight_copies()
        copies[2].wait()
        copies[3].wait()

    pl.when(i < n_tiles)(ffn_tail)


def _ffn_mixer(xa, xb, xb_tile0, cc_in, s_in, w, layer, n_prompt_tiles, n_sample_tiles, tiles_per_seq, n_prompt_seq):
    n_tiles = n_prompt_tiles + n_sample_tiles
    sample_tile = lambda t: jnp.clip(t - n_prompt_tiles, 0, n_sample_tiles - 1)
    row = lambda f: pl.BlockSpec((TM, D_MODEL), lambda i: (f(i), 0))
    per_seq = lambda r, c: pl.BlockSpec((N_CHUNK, r, c), lambda i: (sample_tile(i - 1), 0, 0))
    once = pl.Buffered(1)
    per_seq_in = lambda r, c: pl.BlockSpec((None, N_CHUNK, r, c), lambda i: (layer, sample_tile(i - 1), 0, 0),
                                           pipeline_mode=once)
    xb_spec = pl.BlockSpec((TM, D_MODEL), lambda i: (xb_tile0 + sample_tile(i), 0), pipeline_mode=once)
    n_sample_seq = n_sample_tiles * N_CHUNK
    return pl.pallas_call(
        functools.partial(_ffn_mixer_kernel, layer=layer, n_prompt_tiles=n_prompt_tiles, n_tiles=n_tiles,
                          tiles_per_seq=tiles_per_seq),
        grid=(n_tiles + 1,),
        in_specs=[row(lambda i: jnp.minimum(i, n_prompt_tiles - 1)), xb_spec]
        + [_HBM] * 4 + [_VMEM] * 10 + [per_seq_in(HIST, C_CONV), per_seq_in(QK_W, DV)],
        out_specs=[row(lambda i: jnp.minimum(i, n_tiles - 1)), row(lambda i: jnp.maximum(i - 1, 0)),
                   _VMEM, _VMEM, per_seq(HIST, C_CONV), per_seq(QK_W, DV)],
        out_shape=[jax.ShapeDtypeStruct((n_tiles * TM, D_MODEL), F32),
                   jax.ShapeDtypeStruct((n_tiles * TM, D_MODEL), BF16),
                   jax.ShapeDtypeStruct((n_prompt_seq, HIST, C_CONV), F32),
                   jax.ShapeDtypeStruct((n_prompt_seq, QK_W, DV), F32),
                   jax.ShapeDtypeStruct((n_sample_seq, HIST, C_CONV), F32),
                   jax.ShapeDtypeStruct((n_sample_seq, QK_W, DV), F32)],
        scratch_shapes=[pltpu.VMEM((D_MODEL, D_FF), BF16), pltpu.VMEM((D_MODEL, D_FF), BF16),
                        pltpu.VMEM((D_FF, D_MODEL), BF16), pltpu.VMEM((D_MODEL, N_MAIN), BF16),
                        pltpu.SemaphoreType.DMA((4,)),
                        pltpu.VMEM((TM, D_FF), BF16), pltpu.VMEM((TM, D_MODEL), BF16),
                        pltpu.VMEM((N_SLAB, N_CHUNK * SPAN, LANES), F32), pltpu.VMEM((N_SLAB, HIST, LANES), F32),
                        pltpu.VMEM((TM, 2 * QK_W), F32), pltpu.VMEM((TM, V_W), BF16),
                        pltpu.VMEM((TM, V_W), F32), pltpu.VMEM((TM, QK_W), F32),
                        pltpu.VMEM((QK_W, DV), F32), pltpu.VMEM((QK_W, DV), F32), pltpu.VMEM((CHUNK, QK_W), F32),
                        pltpu.VMEM((D_MODEL, QK_W), BF16)],
        compiler_params=pltpu.CompilerParams(dimension_semantics=("arbitrary",), vmem_limit_bytes=VMEM_LIMIT),
        name="ffn_mixer",
    )(xa, xb, w["wg1"], w["wu1"], w["wd1"], w["win"], w["n1"][layer], w["n2"][layer], w["wgz"][layer],
      w["w2"][layer], w["gb"][layer], w["dw"][layer], w["dwb"][layer], w["lng"][layer], w["lnb"][layer],
      w["gng"][layer], cc_in, s_in)


def _outproj_ffn_kernel(x1_ref, m_ref, wo_hbm, wg_hbm, wu_hbm, wd_hbm, n3_ref, nf_ref, *rest,
                        layer, final, n_prompt_tiles):
    if final:
        yp_ref, ys_ref, wo_ref, wg_ref, wu_ref, wd_ref, sem, a_ref = rest
    else:
        o_ref, wo_ref, wg_ref, wu_ref, wd_ref, sem, a_ref = rest
    i = pl.program_id(0)

    @pl.when(i == 0)
    def _():
        copies = _weight_copies(((wo_hbm.at[layer], wo_ref), (wg_hbm.at[layer], wg_ref),
                                 (wu_hbm.at[layer], wu_ref), (wd_hbm.at[layer], wd_ref)), sem)
        for c in copies:
            c.start()
        for c in copies:
            c.wait()

    x2 = x1_ref[...] + _dot(m_ref[...], wo_ref[...])
    h = _rms(x2, n3_ref[...]).astype(BF16)
    _gate_up(h, wg_ref, wu_ref, a_ref, 0, D_FF)
    x3 = x2 + 0.5 * _dot(a_ref[...], wd_ref[...])
    if final:
        y = _rms(x3, nf_ref[...])

        @pl.when(i < n_prompt_tiles)
        def _():
            yp_ref[...] = y

        @pl.when(i >= n_prompt_tiles)
        def _():
            ys_ref[...] = y
    else:
        o_ref[...] = x3


def _outproj_ffn(x1, mix, w, layer, final, n_prompt_tiles, n_sample_tiles):
    n_tiles = n_prompt_tiles + n_sample_tiles
    row = lambda f: pl.BlockSpec((TM, D_MODEL), lambda i: (f(i), 0))
    if final:
        out_specs = [row(lambda i: jnp.minimum(i, n_prompt_tiles - 1)),
                     row(lambda i: jnp.clip(i - n_prompt_tiles, 0, n_sample_tiles - 1))]
        out_shape = [jax.ShapeDtypeStruct((n_prompt_tiles * TM, D_MODEL), F32),
                     jax.ShapeDtypeStruct((n_sample_tiles * TM, D_MODEL), F32)]
    else:
        out_specs = row(lambda i: i)
        out_shape = jax.ShapeDtypeStruct((n_tiles * TM, D_MODEL), F32)
    return pl.pallas_call(
        functools.partial(_outproj_ffn_kernel, layer=layer, final=final, n_prompt_tiles=n_prompt_tiles),
        grid=(n_tiles,),
        in_specs=[row(lambda i: i), row(lambda i: i)] + [_HBM] * 4 + [_VMEM] * 2,
        out_specs=out_specs,
        out_shape=out_shape,
        scratch_shapes=[pltpu.VMEM((D_MODEL, D_MODEL), BF16), pltpu.VMEM((D_MODEL, D_FF), BF16),
                        pltpu.VMEM((D_MODEL, D_FF), BF16), pltpu.VMEM((D_FF, D_MODEL), BF16),
                        pltpu.SemaphoreType.DMA((4,)),
                        pltpu.VMEM((TM, D_FF), BF16)],
        compiler_params=pltpu.CompilerParams(dimension_semantics=("arbitrary",), vmem_limit_bytes=VMEM_LIMIT),
        name="outproj_ffn",
    )(x1, mix, w["wo"], w["wg2"], w["wu2"], w["wd2"], w["n3"][layer], w["nf"])


def kernel(x_prompt, x_sample, cache_conv, state_gla, ffn1_norm, ffn1_w_gate, ffn1_w_up, ffn1_w_down, mix_norm, w_in, conv_dw_w, conv_dw_b, conv_ln_g, conv_ln_b, gla_gate_w2, gla_gate_b, gla_norm_g, w_out, ffn2_norm, ffn2_w_gate, ffn2_w_up, ffn2_w_down, final_norm):
    bp, lp, _ = x_prompt.shape
    bs, ls, _ = x_sample.shape
    assert lp % TM == 0 and ls == CHUNK and (bs * ls) % TM == 0
    n_prompt_tiles, n_sample_tiles, tiles_per_seq = bp * lp // TM, bs * ls // TM, lp // TM

    layers = lambda f: [f(l) for l in range(DEPTH)]
    row = lambda a: layers(lambda l: a[l].reshape(1, a.shape[-1]))
    w = dict(
        wg1=ffn1_w_gate.astype(BF16), wu1=ffn1_w_up.astype(BF16), wd1=ffn1_w_down.astype(BF16),
        win=w_in.astype(BF16), wo=w_out.astype(BF16), wg2=ffn2_w_gate.astype(BF16),
        wu2=ffn2_w_up.astype(BF16), wd2=ffn2_w_down.astype(BF16), n1=row(ffn1_norm), n2=row(mix_norm), n3=row(ffn2_norm), nf=final_norm.reshape(1, D_MODEL),
        wgz=layers(lambda l: jnp.pad(w_in[l, :, N_MAIN:], ((0, 0), (0, LANES - GATE_RANK)))),
        w2=layers(lambda l: jnp.pad(gla_gate_w2[l], ((0, LANES - GATE_RANK), (0, 0)))), gb=row(gla_gate_b),
        dw=layers(lambda l: jnp.pad(conv_dw_w[l], ((0, 1), (0, 0)))), dwb=row(conv_dw_b), lng=row(conv_ln_g),
        lnb=row(conv_ln_b), gng=row(gla_norm_g),
    )
    cc_in = jnp.pad(cache_conv, ((0, 0), (0, 0), (HIST - (CONV_W - 1), 0), (0, 0)))
    s_in = state_gla.reshape(DEPTH, bs, QK_W, DV)

    xa, xb, xb_tile0 = x_prompt.reshape(bp * lp, D_MODEL), x_sample.reshape(bs * ls, D_MODEL), 0
    cc_p, s_p, cc_s, s_s = [], [], [], []
    for l in range(DEPTH):
        x1, mix, ccp, sp, ccs, ss = _ffn_mixer(xa, xb, xb_tile0, cc_in, s_in, w, l, n_prompt_tiles,
                                               n_sample_tiles, tiles_per_seq, bp)
        cc_p.append(ccp[:, HIST - (CONV_W - 1):, :])
        cc_s.append(ccs[:, HIST - (CONV_W - 1):, :])
        s_p.append(sp.reshape(bp, HEADS, DK, DV))
        s_s.append(ss.reshape(bs, HEADS, DK, DV))
        out = _outproj_ffn(x1, mix, w, l, l == DEPTH - 1, n_prompt_tiles, n_sample_tiles)
        xa, xb, xb_tile0 = out, out, n_prompt_tiles
    y_p, y_s = out
    return (y_p.reshape(bp, lp, D_MODEL), y_s.reshape(bs, ls, D_MODEL), jnp.stack(cc_p), jnp.stack(s_p),
            jnp.stack(cc_s), jnp.stack(s_s))
```

```python
import functools

import jax
import jax.numpy as jnp
from jax import lax
from jax.experimental import pallas as pl
from jax.experimental.pallas import tpu as pltpu

D_MODEL = 1024
DEPTH = 2
C_CONV = 512
CONV_W = 31
HEADS = 4
DK = 64
DV = 128
QK_W = HEADS * DK
V_W = HEADS * DV
N_MAIN = 2 * C_CONV + 2 * QK_W + 2 * V_W
GATE_RANK = 16
GATE_TAU = 16.0
D_FF = 2816
EPS = 1e-6
CHUNK = 64
SUB = 16
NSUB = CHUNK // SUB
FAST_LOG_DECAY_MIN = -4.0

LANES = 128
N_SLAB = C_CONV // LANES
HIST = 32
SPAN = HIST + CHUNK
FF_CHUNK = 256
TM = 512
N_CHUNK = TM // CHUNK
VMEM_LIMIT = 58 * 1024 * 1024

F32 = jnp.float32
BF16 = jnp.bfloat16

_VMEM = pl.BlockSpec(memory_space=pltpu.VMEM)
_HBM = pl.BlockSpec(memory_space=pl.ANY)


def _sigmoid(x):
    return 1.0 / (1.0 + jnp.exp(-x))


def _rms(x, g):
    return x * lax.rsqrt(jnp.mean(x * x, axis=-1, keepdims=True) + EPS) * g


def _dot(a, b):
    return jnp.dot(a, b, preferred_element_type=F32)


def _weight_copies(pairs, sem):
    return [pltpu.make_async_copy(src, dst, sem.at[k]) for k, (src, dst) in enumerate(pairs)]


def _gate_up(h, wg_ref, wu_ref, a_ref, lo, hi, between=None):
    for c in range(lo, hi, FF_CHUNK):
        if between is not None:
            next(between, None)
        sl = slice(c, c + FF_CHUNK)
        g = _dot(h, wg_ref[:, sl])
        u = _dot(h, wu_ref[:, sl])
        a_ref[:, sl] = (g * _sigmoid(g) * u).astype(BF16)


def _gla_cumsum(la):
    r = lax.broadcasted_iota(jnp.int32, (CHUNK, CHUNK), 0)
    c = lax.broadcasted_iota(jnp.int32, (CHUNK, CHUNK), 1)
    tril = (r >= c).astype(BF16)
    la_hi = la.astype(BF16)
    r1 = la - la_hi.astype(F32)
    la_mid = r1.astype(BF16)
    la_lo = (r1 - la_mid.astype(F32)).astype(BF16)
    return _dot(jnp.concatenate([tril] * 3, axis=1), jnp.concatenate([la_hi, la_mid, la_lo], axis=0))


def _gla_operands(q, k, b):
    lane_head = lax.broadcasted_iota(jnp.int32, (CHUNK, QK_W), 1) // DK
    row_head = lax.broadcasted_iota(jnp.int32, (CHUNK, QK_W), 0) // SUB
    head_mask = lane_head == row_head
    j_idx = lax.broadcasted_iota(jnp.int32, (CHUNK, QK_W), 0)
    b_last = b[CHUNK - 1:CHUNK, :]
    qe = q * jnp.exp(b)
    qxs, kxs = [], []
    for i in range(NSUB):
        rows = slice(i * SUB, (i + 1) * SUB)
        b_ref = b[i * SUB - 1:i * SUB, :] if i > 0 else jnp.zeros((1, QK_W), F32)
        qt = q[rows] * jnp.exp(b[rows] - b_ref)
        qxs.append(jnp.where(head_mask, jnp.concatenate([qt] * HEADS, axis=0), 0.0).astype(BF16))
        kxs.append(jnp.where(j_idx < (i + 1) * SUB, k * jnp.exp(b_ref - b), 0.0).astype(BF16))
    qe_all = qe.astype(BF16)
    kd = (k * jnp.exp(b_last - b)).astype(BF16)
    decay = jnp.exp(jnp.broadcast_to(b_last, (DV, QK_W)).T)
    return qxs, kxs, qe_all, kd, decay


def _gla_scores(qxs, kxs):
    r = lax.broadcasted_iota(jnp.int32, (CHUNK, CHUNK), 0)
    c = lax.broadcasted_iota(jnp.int32, (CHUNK, CHUNK), 1)
    row_local = r % SUB
    p_blocks = []
    for i in range(NSUB):
        a = lax.dot_general(qxs[i], kxs[i], (((1,), (1,)), ((), ())), preferred_element_type=F32)
        p_blocks.append(jnp.where(c <= i * SUB + row_local, a, 0.0).astype(BF16))
    return jnp.concatenate(p_blocks, axis=0)


def _gla_update(kd, v):
    kdt = kd.T
    return jnp.concatenate([_dot(kdt[h * DK:(h + 1) * DK, :], v[:, h * DV:(h + 1) * DV]) for h in range(HEADS)],
                           axis=0)


def _gla_output(p_all, qe_all, v, s):
    sb = s.astype(BF16)
    heads = []
    for h in range(HEADS):
        head_rows = [slice(i * CHUNK + h * SUB, i * CHUNK + (h + 1) * SUB) for i in range(NSUB)]
        ph = jnp.concatenate([p_all[r] for r in head_rows], axis=0)
        lhs = jnp.concatenate([ph, qe_all[:, h * DK:(h + 1) * DK]], axis=1)
        rhs = jnp.concatenate([v[:, h * DV:(h + 1) * DV], sb[h * DK:(h + 1) * DK, :]], axis=0)
        heads.append(_dot(lhs, rhs))
    return jnp.concatenate(heads, axis=1)


def _gla_tile(state_in, store_state, qk_scr, v_scr, sg_scr, la_scr, gng_ref, m_ref):
    rows = [slice(c * CHUNK, (c + 1) * CHUNK) for c in range(N_CHUNK)]
    bs = [_gla_cumsum(la_scr[r, :]) for r in rows]
    yield
    ops = [_gla_operands(qk_scr[r, 0:QK_W], qk_scr[r, QK_W:2 * QK_W], b) for r, b in zip(rows, bs)]
    ps = [_gla_scores(o[0], o[1]) for o in ops]
    upds = [_gla_update(o[3], v_scr[r, :]) for r, o in zip(rows, ops)]
    yield
    s_in, s_prev = [], None
    for c in range(N_CHUNK):
        s_in.append(state_in(c, s_prev))
        s_prev = ops[c][4] * s_in[c] + upds[c]
        store_state(c, s_prev)
    outs = [_gla_output(ps[c], ops[c][2], v_scr[rows[c], :], s_in[c]) for c in range(N_CHUNK)]
    yield
    for c in range(N_CHUNK):
        for h in range(HEADS):
            hs = slice(h * DV, (h + 1) * DV)
            oh = outs[c][:, hs]
            on = oh * lax.rsqrt(jnp.mean(oh * oh, axis=-1, keepdims=True) + EPS) * gng_ref[:, hs]
            m_ref[rows[c], C_CONV + h * DV:C_CONV + (h + 1) * DV] = (on * sg_scr[rows[c], hs]).astype(BF16)
    yield


def _gla_scores_exact(q, k_row, b, b_scr):
    b_scr[...] = b
    lane_head = lax.broadcasted_iota(jnp.int32, (CHUNK, QK_W), 1) // DK
    row_head = lax.broadcasted_iota(jnp.int32, (CHUNK, QK_W), 0) // SUB
    head_mask = lane_head == row_head
    q_rows, b_rows = [], []
    for i in range(NSUB):
        rows = slice(i * SUB, (i + 1) * SUB)
        q_rows.append(jnp.where(head_mask, jnp.concatenate([q[rows]] * HEADS, axis=0), 0.0))
        b_rows.append(jnp.concatenate([b[rows]] * HEADS, axis=0))
    q_rows = jnp.concatenate(q_rows, axis=0)
    b_rows = jnp.concatenate(b_rows, axis=0)
    n_rows = NSUB * CHUNK
    col = lax.broadcasted_iota(jnp.int32, (n_rows, LANES), 1)
    r = lax.broadcasted_iota(jnp.int32, (n_rows, LANES), 0)
    token = r // CHUNK * SUB + r % SUB

    def body(j, acc):
        p = q_rows * k_row(j) * jnp.exp(jnp.minimum(b_rows - b_scr[pl.ds(j, 1), :], 0.0))
        return jnp.where(col == j, jnp.sum(p, axis=-1, keepdims=True), acc)

    acc = lax.fori_loop(0, CHUNK, body, jnp.zeros((n_rows, LANES), F32))
    return jnp.where(col <= token, acc, 0.0)[:, 0:CHUNK].astype(BF16)


def _gla_tile_exact(state_in, store_state, qk_scr, v_scr, sg_scr, la_scr, gng_ref, m_ref, b_scr):
    s_prev = None
    for c in range(N_CHUNK):
        rows = slice(c * CHUNK, (c + 1) * CHUNK)
        q = qk_scr[rows, 0:QK_W]
        b = _gla_cumsum(la_scr[rows, :])
        _, _, qe_all, kd, decay = _gla_operands(q, qk_scr[rows, QK_W:2 * QK_W], b)
        p_all = _gla_scores_exact(q, lambda j: qk_scr[pl.ds(c * CHUNK + j, 1), QK_W:2 * QK_W], b, b_scr)
        s_in = state_in(c, s_prev)
        s_prev = decay * s_in + _gla_update(kd, v_scr[rows, :])
        store_state(c, s_prev)
        o = _gla_output(p_all, qe_all, v_scr[rows, :], s_in)
        for h in range(HEADS):
            hs = slice(h * DV, (h + 1) * DV)
            oh = o[:, hs]
            on = oh * lax.rsqrt(jnp.mean(oh * oh, axis=-1, keepdims=True) + EPS) * gng_ref[:, hs]
            m_ref[rows, C_CONV + h * DV:C_CONV + (h + 1) * DV] = (on * sg_scr[rows, hs]).astype(BF16)


def _conv_chunk(zrow, rows, zbuf, dw_ref, dwb_ref, lng_ref, lnb_ref, m_ref):
    acc = []
    for j in range(N_SLAB):
        ls = slice(j * LANES, (j + 1) * LANES)
        a = jnp.broadcast_to(dwb_ref[:, ls], (CHUNK, LANES))
        for w in range(CONV_W):
            a = a + zbuf[j, pl.ds(zrow - (CONV_W - 1) + w, CHUNK, stride=1), :] * dw_ref[w:w + 1, ls]
        acc.append(a)
    mu = sum(jnp.sum(a, axis=-1, keepdims=True) for a in acc) * (1.0 / C_CONV)
    xc = [a - mu for a in acc]
    rstd = lax.rsqrt(sum(jnp.sum(x * x, axis=-1, keepdims=True) for x in xc) * (1.0 / C_CONV) + EPS)
    for j in range(N_SLAB):
        ls = slice(j * LANES, (j + 1) * LANES)
        yn = xc[j] * rstd * lng_ref[:, ls] + lnb_ref[:, ls]
        m_ref[rows, ls] = (yn * _sigmoid(yn)).astype(BF16)


def _ffn_mixer_kernel(xa_ref, xb_ref, wg_hbm, wu_hbm, wd_hbm, win_hbm,
                      n1_ref, n2_ref, wgz_ref, w2_ref, gb_ref, dw_ref, dwb_ref, lng_ref, lnb_ref, gng_ref,
                      cci_ref, si_ref,
                      x1_ref, m_ref, ccp_ref, sp_ref, ccs_ref, ss_ref,
                      wg_ref, wu_ref, wd_ref, win_ref, sem, a_ref, h_ref, zbuf, hcarry,
                      qk_scr, v_scr, sg_scr, la_scr, s_scr, s_next, b_scr, wgate_ref,
                      *, layer, n_prompt_tiles, n_tiles, tiles_per_seq):
    i = pl.program_id(0)
    j = jnp.maximum(i - 1, 0)
    ffn_is_prompt = i < n_prompt_tiles
    mix_is_prompt = j < n_prompt_tiles
    seq_tile = j % tiles_per_seq

    def conv_work():
        for c in range(N_CHUNK):
            for sl in range(N_SLAB):
                prev = hcarry[sl] if c == 0 else zbuf[sl, (c - 1) * SPAN + CHUNK:c * SPAN, :]
                zbuf[sl, c * SPAN:c * SPAN + HIST, :] = jnp.where(
                    mix_is_prompt, prev, cci_ref[c, :, sl * LANES:(sl + 1) * LANES])
            _conv_chunk(c * SPAN + HIST, slice(c * CHUNK, (c + 1) * CHUNK), zbuf,
                        dw_ref, dwb_ref, lng_ref, lnb_ref, m_ref)
            for sl in range(N_SLAB):
                ccs_ref[c, :, sl * LANES:(sl + 1) * LANES] = zbuf[sl, c * SPAN + CHUNK:(c + 1) * SPAN, :]

    def state_in(c, s_prev):
        return jnp.where(mix_is_prompt, s_scr[...] if c == 0 else s_prev, si_ref[c])

    def store_state(c, s):
        ss_ref[c] = s
        if c == N_CHUNK - 1:
            s_next[...] = s

    def gla_work():
        return _gla_tile(state_in, store_state, qk_scr, v_scr, sg_scr, la_scr, gng_ref, m_ref)

    def x_tile():
        return jnp.where(ffn_is_prompt, xa_ref[...], xb_ref[...])

    def mixer_begin():
        @pl.when(mix_is_prompt & (seq_tile == 0))
        def _():
            hcarry[...] = jnp.zeros(hcarry.shape, F32)
            s_scr[...] = jnp.zeros(s_scr.shape, F32)

    def mixer_end(need_exact):
        @pl.when(need_exact)
        def _():
            _gla_tile_exact(state_in, store_state, qk_scr, v_scr, sg_scr, la_scr, gng_ref, m_ref, b_scr)

        s_scr[...] = s_next[...]
        hcarry[...] = zbuf[:, (N_CHUNK - 1) * SPAN + CHUNK:N_CHUNK * SPAN, :]

        @pl.when(mix_is_prompt & (seq_tile == tiles_per_seq - 1))
        def _():
            b = j // tiles_per_seq
            sp_ref[b] = s_scr[...]
            for sl in range(N_SLAB):
                ccp_ref[b, :, sl * LANES:(sl + 1) * LANES] = hcarry[sl]

    def gate_up_region():
        h_ref[...] = _rms(x_tile(), n1_ref[...]).astype(BF16)

        @pl.when(i >= 0)
        def _():
            gen = gla_work()
            _gate_up(h_ref[...], wg_ref, wu_ref, a_ref, 0, D_FF, gen)
            for _ in gen:
                pass
            conv_work()

    def ffn_tail():
        x1 = x_tile() + 0.5 * _dot(a_ref[...], wd_ref[...])
        x1_ref[...] = x1
        h2 = _rms(x1, n2_ref[...]).astype(BF16)
        ua = _dot(h2, win_ref[:, 0:C_CONV])
        ub = _dot(h2, win_ref[:, C_CONV:2 * C_CONV])
        z = ua * _sigmoid(ub)
        for c in range(N_CHUNK):
            for sl in range(N_SLAB):
                zbuf[sl, c * SPAN + HIST:(c + 1) * SPAN, :] = z[c * CHUNK:(c + 1) * CHUNK, sl * LANES:(sl + 1) * LANES]
        o = 2 * C_CONV
        qk_scr[:, 0:QK_W] = _dot(h2, win_ref[:, o:o + QK_W]) * (DK ** -0.5)
        qk_scr[:, QK_W:2 * QK_W] = _dot(h2, win_ref[:, o + QK_W:o + 2 * QK_W])
        o += 2 * QK_W
        v_scr[...] = _dot(h2, win_ref[:, o:o + V_W]).astype(BF16)
        g = _dot(h2, win_ref[:, o + V_W:o + 2 * V_W])
        sg_scr[...] = g * _sigmoid(g)
        pre = _dot(h2, wgate_ref[...]) + gb_ref[...]
        la_scr[...] = (jnp.minimum(pre, 0.0) - jnp.log(1.0 + jnp.exp(-jnp.abs(pre)))) * (1.0 / GATE_TAU)

    def weight_copies():
        return _weight_copies(((wg_hbm.at[layer], wg_ref), (wu_hbm.at[layer], wu_ref), (wd_hbm.at[layer], wd_ref),
                               (win_hbm.at[layer, :, pl.ds(0, N_MAIN)], win_ref)), sem)

    @pl.when(i == 0)
    def _():
        copies = weight_copies()
        for c in copies:
            c.start()
        for ref in (zbuf, hcarry, qk_scr, v_scr, sg_scr, la_scr, s_scr, s_next):
            ref[...] = jnp.zeros(ref.shape, ref.dtype)
        wgate_ref[...] = jnp.dot(wgz_ref[...], w2_ref[...], preferred_element_type=F32,
                                 precision=lax.Precision.HIGHEST).astype(BF16)
        copies[0].wait()
        copies[1].wait()

    need_exact = jnp.min(la_scr[...]) < FAST_LOG_DECAY_MIN
    mixer_begin()
    gate_up_region()
    mixer_end(need_exact)

    @pl.when(i == 0)
    def _():
        copies = weight_copies()
        copies[2].wait()
        copies[3].wait()

    pl.when(i < n_tiles)(ffn_tail)


def _ffn_mixer(xa, xb, xb_tile0, cc_in, s_in, w, layer, n_prompt_tiles, n_sample_tiles, tiles_per_seq, n_prompt_seq):
    n_tiles = n_prompt_tiles + n_sample_tiles
    sample_tile = lambda t: jnp.clip(t - n_prompt_tiles, 0, n_sample_tiles - 1)
    row = lambda f: pl.BlockSpec((TM, D_MODEL), lambda i: (f(i), 0))
    per_seq = lambda r, c: pl.BlockSpec((N_CHUNK, r, c), lambda i: (sample_tile(i - 1), 0, 0))
    once = pl.Buffered(1)
    per_seq_in = lambda r, c: pl.BlockSpec((None, N_CHUNK, r, c), lambda i: (layer, sample_tile(i - 1), 0, 0),
                                           pipeline_mode=once)
    xb_spec = pl.BlockSpec((TM, D_MODEL), lambda i: (xb_tile0 + sample_tile(i), 0), pipeline_mode=once)
    n_sample_seq = n_sample_tiles * N_CHUNK
    return pl.pallas_call(
        functools.partial(_ffn_mixer_kernel, layer=layer, n_prompt_tiles=n_prompt_tiles, n_tiles=n_tiles,
                          tiles_per_seq=tiles_per_seq),
        grid=(n_tiles + 1,),
        in_specs=[row(lambda i: jnp.minimum(i, n_prompt_tiles - 1)), xb_spec]
        + [_HBM] * 4 + [_VMEM] * 10 + [per_seq_in(HIST, C_CONV), per_seq_in(QK_W, DV)],
        out_specs=[row(lambda i: jnp.minimum(i, n_tiles - 1)), row(lambda i: jnp.maximum(i - 1, 0)),
                   _VMEM, _VMEM, per_seq(HIST, C_CONV), per_seq(QK_W, DV)],
        out_shape=[jax.ShapeDtypeStruct((n_tiles * TM, D_MODEL), F32),
                   jax.ShapeDtypeStruct((n_tiles * TM, D_MODEL), BF16),
                   jax.ShapeDtypeStruct((n_prompt_seq, HIST, C_CONV), F32),
                   jax.ShapeDtypeStruct((n_prompt_seq, QK_W, DV), F32),
                   jax.ShapeDtypeStruct((n_sample_seq, HIST, C_CONV), F32),
                   jax.ShapeDtypeStruct((n_sample_seq, QK_W, DV), F32)],
        scratch_shapes=[pltpu.VMEM((D_MODEL, D_FF), BF16), pltpu.VMEM((D_MODEL, D_FF), BF16),
                        pltpu.VMEM((D_FF, D_MODEL), BF16), pltpu.VMEM((D_MODEL, N_MAIN), BF16),
                        pltpu.SemaphoreType.DMA((4,)),
                        pltpu.VMEM((TM, D_FF), BF16), pltpu.VMEM((TM, D_MODEL), BF16),
                        pltpu.VMEM((N_SLAB, N_CHUNK * SPAN, LANES), F32), pltpu.VMEM((N_SLAB, HIST, LANES), F32),
                        pltpu.VMEM((TM, 2 * QK_W), F32), pltpu.VMEM((TM, V_W), BF16),
                        pltpu.VMEM((TM, V_W), F32), pltpu.VMEM((TM, QK_W), F32),
                        pltpu.VMEM((QK_W, DV), F32), pltpu.VMEM((QK_W, DV), F32), pltpu.VMEM((CHUNK, QK_W), F32),
                        pltpu.VMEM((D_MODEL, QK_W), BF16)],
        compiler_params=pltpu.CompilerParams(dimension_semantics=("arbitrary",), vmem_limit_bytes=VMEM_LIMIT),
        name="ffn_mixer",
    )(xa, xb, w["wg1"], w["wu1"], w["wd1"], w["win"], w["n1"][layer], w["n2"][layer], w["wgz"][layer],
      w["w2"][layer], w["gb"][layer], w["dw"][layer], w["dwb"][layer], w["lng"][layer], w["lnb"][layer],
      w["gng"][layer], cc_in, s_in)


def _outproj_ffn_kernel(x1_ref, m_ref, wo_hbm, wg_hbm, wu_hbm, wd_hbm, n3_ref, nf_ref, *rest,
                        layer, final, n_prompt_tiles):
    if final:
        yp_ref, ys_ref, wo_ref, wg_ref, wu_ref, wd_ref, sem, a_ref = rest
    else:
        o_ref, wo_ref, wg_ref, wu_ref, wd_ref, sem, a_ref = rest
    i = pl.program_id(0)

    @pl.when(i == 0)
    def _():
        copies = _weight_copies(((wo_hbm.at[layer], wo_ref), (wg_hbm.at[layer], wg_ref),
                                 (wu_hbm.at[layer], wu_ref), (wd_hbm.at[layer], wd_ref)), sem)
        for c in copies:
            c.start()
        for c in copies:
            c.wait()

    x2 = x1_ref[...] + _dot(m_ref[...], wo_ref[...])
    h = _rms(x2, n3_ref[...]).astype(BF16)
    _gate_up(h, wg_ref, wu_ref, a_ref, 0, D_FF)
    x3 = x2 + 0.5 * _dot(a_ref[...], wd_ref[...])
    if final:
        y = _rms(x3, nf_ref[...])

        @pl.when(i < n_prompt_tiles)
        def _():
            yp_ref[...] = y

        @pl.when(i >= n_prompt_tiles)
        def _():
            ys_ref[...] = y
    else:
        o_ref[...] = x3


def _outproj_ffn(x1, mix, w, layer, final, n_prompt_tiles, n_sample_tiles):
    n_tiles = n_prompt_tiles + n_sample_tiles
    row = lambda f: pl.BlockSpec((TM, D_MODEL), lambda i: (f(i), 0))
    if final:
        out_specs = [row(lambda i: jnp.minimum(i, n_prompt_tiles - 1)),
                     row(lambda i: jnp.clip(i - n_prompt_tiles, 0, n_sample_tiles - 1))]
        out_shape = [jax.ShapeDtypeStruct((n_prompt_tiles * TM, D_MODEL), F32),
                     jax.ShapeDtypeStruct((n_sample_tiles * TM, D_MODEL), F32)]
    else:
        out_specs = row(lambda i: i)
        out_shape = jax.ShapeDtypeStruct((n_tiles * TM, D_MODEL), F32)
    return pl.pallas_call(
        functools.partial(_outproj_ffn_kernel, layer=layer, final=final, n_prompt_tiles=n_prompt_tiles),
        grid=(n_tiles,),
        in_specs=[row(lambda i: i), row(lambda i: i)] + [_HBM] * 4 + [_VMEM] * 2,
        out_specs=out_specs,
        out_shape=out_shape,
        scratch_shapes=[pltpu.VMEM((D_MODEL, D_MODEL), BF16), pltpu.VMEM((D_MODEL, D_FF), BF16),
                        pltpu.VMEM((D_MODEL, D_FF), BF16), pltpu.VMEM((D_FF, D_MODEL), BF16),
                        pltpu.SemaphoreType.DMA((4,)),
                        pltpu.VMEM((TM, D_FF), BF16)],
        compiler_params=pltpu.CompilerParams(dimension_semantics=("arbitrary",), vmem_limit_bytes=VMEM_LIMIT),
        name="outproj_ffn",
    )(x1, mix, w["wo"], w["wg2"], w["wu2"], w["wd2"], w["n3"][layer], w["nf"])


def kernel(x_prompt, x_sample, cache_conv, state_gla, ffn1_norm, ffn1_w_gate, ffn1_w_up, ffn1_w_down, mix_norm, w_in, conv_dw_w, conv_dw_b, conv_ln_g, conv_ln_b, gla_gate_w2, gla_gate_b, gla_norm_g, w_out, ffn2_norm, ffn2_w_gate, ffn2_w_up, ffn2_w_down, final_norm):
    bp, lp, _ = x_prompt.shape
    bs, ls, _ = x_sample.shape
    assert lp % TM == 0 and ls == CHUNK and (bs * ls) % TM == 0
    n_prompt_tiles, n_sample_tiles, tiles_per_seq = bp * lp // TM, bs * ls // TM, lp // TM

    layers = lambda f: [f(l) for l in range(DEPTH)]
    row = lambda a: layers(lambda l: a[l].reshape(1, a.shape[-1]))
    w = dict(
        wg1=ffn1_w_gate.astype(BF16), wu1=ffn1_w_up.astype(BF16), wd1=ffn1_w_down.astype(BF16),
        win=w_in.astype(BF16), wo=w_out.astype(BF16), wg2=ffn2_w_gate.astype(BF16),
        wu2=ffn2_w_up.astype(BF16), wd2=ffn2_w_down.astype(BF16), n1=row(ffn1_norm), n2=row(mix_norm), n3=row(ffn2_norm), nf=final_norm.reshape(1, D_MODEL),
        wgz=layers(lambda l: jnp.pad(w_in[l, :, N_MAIN:], ((0, 0), (0, LANES - GATE_RANK)))),
        w2=layers(lambda l: jnp.pad(gla_gate_w2[l], ((0, LANES - GATE_RANK), (0, 0)))), gb=row(gla_gate_b),
        dw=layers(lambda l: jnp.pad(conv_dw_w[l], ((0, 1), (0, 0)))), dwb=row(conv_dw_b), lng=row(conv_ln_g),
        lnb=row(conv_ln_b), gng=row(gla_norm_g),
    )
    cc_in = jnp.pad(cache_conv, ((0, 0), (0, 0), (HIST - (CONV_W - 1), 0), (0, 0)))
    s_in = state_gla.reshape(DEPTH, bs, QK_W, DV)

    xa, xb, xb_tile0 = x_prompt.reshape(bp * lp, D_MODEL), x_sample.reshape(bs * ls, D_MODEL), 0
    cc_p, s_p, cc_s, s_s = [], [], [], []
    for l in range(DEPTH):
        x1, mix, ccp, sp, ccs, ss = _ffn_mixer(xa, xb, xb_tile0, cc_in, s_in, w, l, n_prompt_tiles,
                                               n_sample_tiles, tiles_per_seq, bp)
        cc_p.append(ccp[:, HIST - (CONV_W - 1):, :])
        cc_s.append(ccs[:, HIST - (CONV_W - 1):, :])
        s_p.append(sp.reshape(bp, HEADS, DK, DV))
        s_s.append(ss.reshape(bs, HEADS, DK, DV))
        out = _outproj_ffn(x1, mix, w, l, l == DEPTH - 1, n_prompt_tiles, n_sample_tiles)
        xa, xb, xb_tile0 = out, out, n_prompt_tiles
    y_p, y_s = out
    return (y_p.reshape(bp, lp, D_MODEL), y_s.reshape(bs, ls, D_MODEL), jnp.stack(cc_p), jnp.stack(s_p),
            jnp.stack(cc_s), jnp.stack(s_s))
```
